```python
import math
import jax, jax.numpy as jnp
from jax import lax
import numpy as np

D_MODEL = 1024
BATCH = 4
SEQ = 8192
DEPTH = 2

CHUNK = 64
N_MIXERS = 2
N_SSM_LAYERS = (DEPTH + N_MIXERS - 1) // N_MIXERS
N_ATTN_LAYERS = DEPTH // N_MIXERS

SSM_EXPAND = 2
SSM_D_INNER = SSM_EXPAND * D_MODEL
SSM_HEAD_DIM = 64
SSM_HEADS = SSM_D_INNER // SSM_HEAD_DIM
SSM_GROUPS = 8
SSM_HPG = SSM_HEADS // SSM_GROUPS
SSM_STATE = 128
SSM_CONV = 4
SSM_CONV_DIM = SSM_D_INNER + 2 * SSM_GROUPS * SSM_STATE
SSM_IN_DIM = SSM_D_INNER + SSM_CONV_DIM + SSM_HEADS

DIFF_HEAD_DIM = 64
DIFF_HEADS = D_MODEL // (2 * DIFF_HEAD_DIM)
DIFF_V_DIM = 2 * DIFF_HEAD_DIM
Q_BLOCK = 128

FF_DIM = 4 * D_MODEL

DEEPNORM_ALPHA = (2 * DEPTH) ** 0.25
DEEPNORM_BETA = (8 * DEPTH) ** -0.25
EPS = 1e-5

kernel_name = "hybrid_ssd_diffattn_deepnorm_adaln"


def layer_norm(x, g, b):
    xf = x.astype(jnp.float32)
    mu = jnp.mean(xf, axis=-1, keepdims=True)
    xc = xf - mu
    var = jnp.mean(xc * xc, axis=-1, keepdims=True)
    y = xc * lax.rsqrt(var + EPS) * g.astype(jnp.float32) + b.astype(jnp.float32)
    return y.astype(x.dtype)


def causal_depthwise_conv(x, w, b):
    k = w[:, None, :].astype(x.dtype)
    y = lax.conv_general_dilated(x, k, window_strides=(1,), padding=[(SSM_CONV - 1, 0)],
                                 dimension_numbers=("NWC", "WIO", "NWC"),
                                 feature_group_count=x.shape[-1])
    return y + b


def ssd_scan(xdt, a, bm, cm):
    bsz, s = xdt.shape[:2]
    nc = s // CHUNK
    X = xdt.reshape(bsz, nc, CHUNK, SSM_GROUPS, SSM_HPG, SSM_HEAD_DIM)
    A = a.reshape(bsz, nc, CHUNK, SSM_GROUPS, SSM_HPG)
    Bc = bm.reshape(bsz, nc, CHUNK, SSM_GROUPS, SSM_STATE)
    Cc = cm.reshape(bsz, nc, CHUNK, SSM_GROUPS, SSM_STATE)
    A_cs = jnp.cumsum(A, axis=2)
    causal = jnp.tril(jnp.ones((CHUNK, CHUNK), dtype=bool))[:, :, None, None]
    seg = A_cs[:, :, :, None] - A_cs[:, :, None, :]
    decay = jnp.exp(jnp.where(causal, seg, -jnp.inf))
    cb = jnp.einsum('bclgn,bcsgn->bclsg', Cc, Bc)
    y_diag = jnp.einsum('bclsgr,bcsgrp->bclgrp', cb[..., None] * decay, X)
    decay_states = jnp.exp(A_cs[:, :, -1:] - A_cs)
    states = jnp.einsum('bclgn,bclgrp->bcgrpn', Bc, X * decay_states[..., None])
    chunk_decay = jnp.exp(A_cs[:, :, -1])

    def step(h, inp):
        s_c, d_c = inp
        return d_c[..., None, None] * h + s_c, h

    h0 = jnp.zeros_like(states[:, 0])
    _, prev = lax.scan(step, h0, (jnp.moveaxis(states, 1, 0), jnp.moveaxis(chunk_decay, 1, 0)))
    prev = jnp.moveaxis(prev, 0, 1)
    y_off = jnp.einsum('bclgn,bcgrpn->bclgrp', Cc, prev) * jnp.exp(A_cs)[..., None]
    return (y_diag + y_off).reshape(bsz, s, SSM_GROUPS, SSM_HPG, SSM_HEAD_DIM)


def gated_rmsnorm(y, z, w):
    bsz, s, _ = y.shape
    g = (y.astype(jnp.float32) * jax.nn.silu(z.astype(jnp.float32)))
    g = g.reshape(bsz, s, SSM_GROUPS, SSM_D_INNER // SSM_GROUPS)
    g = g * lax.rsqrt(jnp.mean(g * g, axis=-1, keepdims=True) + EPS)
    return (g.reshape(bsz, s, SSM_D_INNER) * w.astype(jnp.float32)).astype(z.dtype)


def ssd_mixer(u, w_in, conv_w, conv_b, dt_bias, a_log, d_skip, norm_w, w_out):
    bsz, s, _ = u.shape
    zxbcdt = u @ w_in
    z = zxbcdt[..., :SSM_D_INNER]
    xbc = zxbcdt[..., SSM_D_INNER:SSM_D_INNER + SSM_CONV_DIM]
    dt = zxbcdt[..., SSM_D_INNER + SSM_CONV_DIM:]
    xbc = jax.nn.silu(causal_depthwise_conv(xbc, conv_w, conv_b))
    xs = xbc[..., :SSM_D_INNER].reshape(bsz, s, SSM_GROUPS, SSM_HPG, SSM_HEAD_DIM)
    bm = xbc[..., SSM_D_INNER:SSM_D_INNER + SSM_GROUPS * SSM_STATE].reshape(bsz, s, SSM_GROUPS, SSM_STATE)
    cm = xbc[..., SSM_D_INNER + SSM_GROUPS * SSM_STATE:].reshape(bsz, s, SSM_GROUPS, SSM_STATE)
    dt = jax.nn.softplus(dt.astype(jnp.float32) + dt_bias.astype(jnp.float32))
    dt = dt.reshape(bsz, s, SSM_GROUPS, SSM_HPG)
    a = -jnp.exp(a_log.astype(jnp.float32)).reshape(SSM_GROUPS, SSM_HPG)
    y = ssd_scan(xs * dt[..., None], dt * a, bm, cm)
    y = y + d_skip.reshape(SSM_GROUPS, SSM_HPG)[:, :, None] * xs
    y = gated_rmsnorm(y.reshape(bsz, s, SSM_D_INNER).astype(u.dtype), z, norm_w)
    return y @ w_out


def diff_attention(u, w_qkv, lq1, lk1, lq2, lk2, subln_w, w_out, lambda_init):
    bsz, s, _ = u.shape
    qkv = u @ w_qkv
    q = qkv[..., :D_MODEL].reshape(bsz, s, DIFF_HEADS, 2, DIFF_HEAD_DIM).transpose(0, 2, 3, 1, 4)
    k = qkv[..., D_MODEL:2 * D_MODEL].reshape(bsz, s, DIFF_HEADS, 2, DIFF_HEAD_DIM).transpose(0, 2, 3, 1, 4)
    v = qkv[..., 2 * D_MODEL:].reshape(bsz, s, DIFF_HEADS, DIFF_V_DIM).transpose(0, 2, 1, 3)
    lam = (jnp.exp(jnp.sum(lq1.astype(jnp.float32) * lk1.astype(jnp.float32)))
           - jnp.exp(jnp.sum(lq2.astype(jnp.float32) * lk2.astype(jnp.float32))) + lambda_init)
    scale = DIFF_HEAD_DIM ** -0.5
    nq = s // Q_BLOCK
    qb = q.reshape(bsz, DIFF_HEADS, 2, nq, Q_BLOCK, DIFF_HEAD_DIM).transpose(3, 0, 1, 2, 4, 5)
    key_chunk = jnp.arange(s) // CHUNK

    def block(args):
        q_blk, idx = args
        sc = jnp.einsum('bhiqd,bhikd->bhiqk', q_blk, k).astype(jnp.float32) * scale
        q_chunk = (idx * Q_BLOCK + jnp.arange(Q_BLOCK)) // CHUNK
        mask = key_chunk[None, :] <= q_chunk[:, None]
        p = jax.nn.softmax(jnp.where(mask, sc, -jnp.inf), axis=-1)
        attn = p[:, :, 0] - lam * p[:, :, 1]
        return jnp.einsum('bhqk,bhkv->bhqv', attn.astype(v.dtype), v)

    o = lax.map(block, (qb, jnp.arange(nq)))
    o = o.transpose(1, 2, 0, 3, 4).reshape(bsz, DIFF_HEADS, s, DIFF_V_DIM).astype(jnp.float32)
    o = o * lax.rsqrt(jnp.mean(o * o, axis=-1, keepdims=True) + EPS)
    o = o * subln_w.astype(jnp.float32) * (1.0 - lambda_init)
    o = o.transpose(0, 2, 1, 3).reshape(bsz, s, DIFF_HEADS * DIFF_V_DIM).astype(u.dtype)
    return o @ w_out


def sq_relu_mlp(u, w1, w2):
    h = jax.nn.relu(u @ w1)
    return (h * h) @ w2


def setup_inputs(seed: int = 0) -> dict:
    key = jax.random.key(seed)
    ks = jax.random.split(key, 32)
    f32 = jnp.float32

    def nrm(k, shape, s):
        return s * jax.random.normal(k, shape, f32)

    NS, NA = N_SSM_LAYERS, N_ATTN_LAYERS
    dt0 = jnp.exp(jax.random.uniform(ks[14], (NS, SSM_HEADS), f32, math.log(1e-3), math.log(1e-1)))
    dt_bias = dt0 + jnp.log(-jnp.expm1(-dt0))
    return {
        "x": nrm(ks[0], (BATCH, SEQ, D_MODEL), 1.0),
        "c": nrm(ks[1], (BATCH, D_MODEL), 1.0),
        "ada_w": nrm(ks[2], (DEPTH, D_MODEL, 6 * D_MODEL), 0.5 * D_MODEL ** -0.5),
        "ada_b": nrm(ks[3], (DEPTH, 6 * D_MODEL), 0.02),
        "ln1_g": 1.0 + nrm(ks[4], (DEPTH, D_MODEL), 0.02),
        "ln1_b": nrm(ks[5], (DEPTH, D_MODEL), 0.02),
        "ln2_g": 1.0 + nrm(ks[6], (DEPTH, D_MODEL), 0.02),
        "ln2_b": nrm(ks[7], (DEPTH, D_MODEL), 0.02),
        "mlp_w1": nrm(ks[8], (DEPTH, D_MODEL, FF_DIM), D_MODEL ** -0.5),
        "mlp_w2": nrm(ks[9], (DEPTH, FF_DIM, D_MODEL), DEEPNORM_BETA * FF_DIM ** -0.5),
        "ssm_w_in": nrm(ks[10], (NS, D_MODEL, SSM_IN_DIM), D_MODEL ** -0.5),
        "ssm_conv_w": nrm(ks[11], (NS, SSM_CONV, SSM_CONV_DIM), SSM_CONV ** -0.5),
        "ssm_conv_b": nrm(ks[12], (NS, SSM_CONV_DIM), 0.02),
        "ssm_dt_bias": dt_bias,
        "ssm_a_log": jnp.log(jax.random.uniform(ks[15], (NS, SSM_HEADS), f32, 1.0, 16.0)),
        "ssm_d": 1.0 + nrm(ks[16], (NS, SSM_HEADS), 0.02),
        "ssm_norm_w": 1.0 + nrm(ks[17], (NS, SSM_D_INNER), 0.02),
        "ssm_w_out": nrm(ks[18], (NS, SSM_D_INNER, D_MODEL), DEEPNORM_BETA * SSM_D_INNER ** -0.5),
        "attn_w_qkv": nrm(ks[19], (NA, D_MODEL, 3 * D_MODEL), D_MODEL ** -0.5),
        "attn_lq1": nrm(ks[20], (NA, DIFF_HEAD_DIM), 0.1),
        "attn_lk1": nrm(ks[21], (NA, DIFF_HEAD_DIM), 0.1),
        "attn_lq2": nrm(ks[22], (NA, DIFF_HEAD_DIM), 0.1),
        "attn_lk2": nrm(ks[23], (NA, DIFF_HEAD_DIM), 0.1),
        "attn_subln_w": 1.0 + nrm(ks[24], (NA, DIFF_V_DIM), 0.02),
        "attn_w_out": nrm(ks[25], (NA, D_MODEL, D_MODEL), DEEPNORM_BETA * D_MODEL ** -0.5),
    }


def reference(x, c, ada_w, ada_b, ln1_g, ln1_b, ln2_g, ln2_b, mlp_w1, mlp_w2,
              ssm_w_in, ssm_conv_w, ssm_conv_b, ssm_dt_bias, ssm_a_log, ssm_d,
              ssm_norm_w, ssm_w_out, attn_w_qkv, attn_lq1, attn_lk1, attn_lq2,
              attn_lk2, attn_subln_w, attn_w_out):
    cond = jax.nn.silu(c)
    for i in range(DEPTH):
        mods = (cond @ ada_w[i] + ada_b[i])[:, None, :]
        sh1, sc1, g1, sh2, sc2, g2 = jnp.split(mods, 6, axis=-1)
        h = x * (1 + sc1) + sh1
        j = i // N_MIXERS
        if i % N_MIXERS == 0:
            y = ssd_mixer(h, ssm_w_in[j], ssm_conv_w[j], ssm_conv_b[j], ssm_dt_bias[j],
                          ssm_a_log[j], ssm_d[j], ssm_norm_w[j], ssm_w_out[j])
        else:
            lambda_init = 0.8 - 0.6 * math.exp(-0.3 * i)
            y = diff_attention(h, attn_w_qkv[j], attn_lq1[j], attn_lk1[j], attn_lq2[j],
                               attn_lk2[j], attn_subln_w[j], attn_w_out[j], lambda_init)
        x = layer_norm(DEEPNORM_ALPHA * x + g1 * y, ln1_g[i], ln1_b[i])
        h = x * (1 + sc2) + sh2
        x = layer_norm(DEEPNORM_ALPHA * x + g2 * sq_relu_mlp(h, mlp_w1[i], mlp_w2[i]), ln2_g[i], ln2_b[i])
    return x
```

```python
import functools
import math

import jax
import jax.numpy as jnp
from jax import lax
from jax.experimental import pallas as pl
from jax.experimental.pallas import tpu as pltpu

F32 = jnp.float32
BF16 = jnp.bfloat16

DEPTH = 2
CHUNK = 64
SSM_HEAD_DIM = 64
SSM_GROUPS = 8
SSM_HPG = 4
SSM_STATE = 128
SSM_CONV = 4
GROUP_W = SSM_HPG * SSM_HEAD_DIM
DIFF_HEAD_DIM = 64
DIFF_V_DIM = 128
DEEPNORM_ALPHA = (2 * DEPTH) ** 0.25
EPS = 1e-5

LANE = 128
SUBLANE = 8
VMEM_LIMIT = 56 * 1024 * 1024


def _cparams(sem):
    return pltpu.CompilerParams(dimension_semantics=sem, vmem_limit_bytes=VMEM_LIMIT)


def _sigmoid(x):
    return 1.0 / (1.0 + jnp.exp(-x))


def _layer_norm(v, g, b):
    mu = jnp.mean(v, axis=-1, keepdims=True)
    vc = v - mu
    var = jnp.mean(vc * vc, axis=-1, keepdims=True)
    return vc * lax.rsqrt(var + EPS) * g + b


def _dot(a, b):
    return jnp.dot(a, b, preferred_element_type=F32)


def _dot_f32(a, b):
    return jnp.dot(a, b, preferred_element_type=F32, precision=lax.Precision.HIGHEST)


def _mods_kernel(c_ref, w_ref, b_ref, o_ref):
    c = c_ref[...]
    cond = c * _sigmoid(c)
    o_ref[0] = _dot(cond.astype(BF16), w_ref[0].astype(BF16)) + b_ref[0]


def _mods(c, ada_w, ada_b):
    depth, d, n = ada_w.shape
    bsz = c.shape[0]
    cp = jnp.zeros((SUBLANE, d), F32).at[:bsz].set(c)
    tn = 1024
    out = pl.pallas_call(
        _mods_kernel,
        grid=(depth, n // tn),
        in_specs=[
            pl.BlockSpec((SUBLANE, d), lambda l, j: (0, 0)),
            pl.BlockSpec((1, d, tn), lambda l, j: (l, 0, j)),
            pl.BlockSpec((1, 1, tn), lambda l, j: (l, 0, j)),
        ],
        out_specs=pl.BlockSpec((1, SUBLANE, tn), lambda l, j: (l, 0, j)),
        out_shape=jax.ShapeDtypeStruct((depth, SUBLANE, n), F32),
        compiler_params=_cparams(("parallel", "parallel")),
        name="mods",
    )(cp, ada_w, ada_b.reshape(depth, 1, n))
    return out[:, :bsz].reshape(depth, bsz * 6, 1, d)


def _proj_kernel(x_ref, sc_ref, sh_ref, w_ref, o_ref, hb_ref):
    @pl.when(pl.program_id(1) == 0)
    def _():
        h = x_ref[...] * (1.0 + sc_ref[0]) + sh_ref[0]
        hb_ref[...] = h.astype(BF16)

    o_ref[...] = _dot(hb_ref[...], w_ref[...]).astype(o_ref.dtype)


def _proj_dt_kernel(x_ref, sc_ref, sh_ref, w_ref, wdt_ref, o_ref, dt_ref, hb_ref):
    @pl.when(pl.program_id(1) == 0)
    def _():
        h = x_ref[...] * (1.0 + sc_ref[0]) + sh_ref[0]
        hb_ref[...] = h.astype(BF16)
        dt_ref[...] = _dot(hb_ref[...], wdt_ref[...])

    o_ref[...] = _dot(hb_ref[...], w_ref[...]).astype(o_ref.dtype)


def _proj(x2, mods, k_scale, k_shift, w, out_dtype, seq, w_dt=None):
    t, d = x2.shape
    n = w.shape[1]
    tm = min(1024, seq)
    tn = min(1024, n)
    per_seq = seq // tm
    in_specs = [
        pl.BlockSpec((tm, d), lambda i, j: (i, 0)),
        pl.BlockSpec((1, 1, d), lambda i, j: ((i // per_seq) * 6 + k_scale, 0, 0)),
        pl.BlockSpec((1, 1, d), lambda i, j: ((i // per_seq) * 6 + k_shift, 0, 0)),
        pl.BlockSpec((d, tn), lambda i, j: (0, j)),
    ]
    scratch = [pltpu.VMEM((tm, d), BF16)]
    cp = _cparams(("parallel", "arbitrary"))
    if w_dt is None:
        return pl.pallas_call(
            _proj_kernel,
            grid=(t // tm, n // tn),
            in_specs=in_specs,
            out_specs=pl.BlockSpec((tm, tn), lambda i, j: (i, j)),
            out_shape=jax.ShapeDtypeStruct((t, n), out_dtype),
            scratch_shapes=scratch,
            compiler_params=cp,
            name="proj",
        )(x2, mods, mods, w)
    ndt = w_dt.shape[1]
    return pl.pallas_call(
        _proj_dt_kernel,
        grid=(t // tm, n // tn),
        in_specs=in_specs + [pl.BlockSpec((d, ndt), lambda i, j: (0, 0))],
        out_specs=[
            pl.BlockSpec((tm, tn), lambda i, j: (i, j)),
            pl.BlockSpec((tm, ndt), lambda i, j: (i, 0)),
        ],
        out_shape=[
            jax.ShapeDtypeStruct((t, n), out_dtype),
            jax.ShapeDtypeStruct((t, ndt), F32),
        ],
        scratch_shapes=scratch,
        compiler_params=cp,
        name="proj_dt",
    )(x2, mods, mods, w, w_dt)


def _outproj_kernel(a_ref, w_ref, x_ref, g_ref, lng_ref, lnb_ref, o_ref):
    y = _dot(a_ref[...], w_ref[...])
    v = DEEPNORM_ALPHA * x_ref[...] + g_ref[0] * y
    o_ref[...] = _layer_norm(v, lng_ref[...], lnb_ref[...])


def _outproj(a, w, x2, mods, k_gate, ln_g, ln_b, seq):
    t, d = x2.shape
    k = a.shape[1]
    tm = min(512, seq)
    per_seq = seq // tm
    return pl.pallas_call(
        _outproj_kernel,
        grid=(t // tm,),
        in_specs=[
            pl.BlockSpec((tm, k), lambda i: (i, 0)),
            pl.BlockSpec((k, d), lambda i: (0, 0)),
            pl.BlockSpec((tm, d), lambda i: (i, 0)),
            pl.BlockSpec((1, 1, d), lambda i: ((i // per_seq) * 6 + k_gate, 0, 0)),
            pl.BlockSpec((1, d), lambda i: (0, 0)),
            pl.BlockSpec((1, d), lambda i: (0, 0)),
        ],
        out_specs=pl.BlockSpec((tm, d), lambda i: (i, 0)),
        out_shape=jax.ShapeDtypeStruct((t, d), F32),
        compiler_params=_cparams(("parallel",)),
        name="outproj",
    )(a, w, x2, mods, ln_g.reshape(1, d), ln_b.reshape(1, d))


def _mlp_kernel(x_ref, sc_ref, sh_ref, g_ref, w1_ref, w2_ref, lng_ref, lnb_ref, o_ref, hb_ref, acc_ref):
    f = pl.program_id(1)

    @pl.when(f == 0)
    def _():
        h = x_ref[...] * (1.0 + sc_ref[0]) + sh_ref[0]
        hb_ref[...] = h.astype(BF16)
        acc_ref[...] = jnp.zeros_like(acc_ref)

    u = jnp.maximum(_dot(hb_ref[...], w1_ref[...]), 0.0)
    acc_ref[...] += _dot((u * u).astype(BF16), w2_ref[...])

    @pl.when(f == pl.num_programs(1) - 1)
    def _():
        v = DEEPNORM_ALPHA * x_ref[...] + g_ref[0] * acc_ref[...]
        o_ref[...] = _layer_norm(v, lng_ref[...], lnb_ref[...])


def _mlp(x2, mods, w1, w2, ln_g, ln_b, seq):
    t, d = x2.shape
    ff = w1.shape[1]
    tm = min(1024, seq)
    tf = min(1024, ff)
    per_seq = seq // tm

    def mod_spec(k):
        return pl.BlockSpec((1, 1, d), lambda i, f: ((i // per_seq) * 6 + k, 0, 0))

    return pl.pallas_call(
        _mlp_kernel,
        grid=(t // tm, ff // tf),
        in_specs=[
            pl.BlockSpec((tm, d), lambda i, f: (i, 0)),
            mod_spec(4), mod_spec(3), mod_spec(5),
            pl.BlockSpec((d, tf), lambda i, f: (0, f)),
            pl.BlockSpec((tf, d), lambda i, f: (f, 0)),
            pl.BlockSpec((1, d), lambda i, f: (0, 0)),
            pl.BlockSpec((1, d), lambda i, f: (0, 0)),
        ],
        out_specs=pl.BlockSpec((tm, d), lambda i, f: (i, 0)),
        out_shape=jax.ShapeDtypeStruct((t, d), F32),
        scratch_shapes=[pltpu.VMEM((tm, d), BF16), pltpu.VMEM((tm, d), F32)],
        compiler_params=_cparams(("parallel", "arbitrary")),
        name="mlp",
    )(x2, mods, mods, mods, w1, w2, ln_g.reshape(1, d), ln_b.reshape(1, d))


def _ssd_kernel(z_ref, x_ref, b_ref, c_ref, dt_ref, wx_ref, wb_ref, wc_ref, bx_ref, bb_ref, bc_ref,
                dtb_ref, alog_ref, dsk_ref, e_ref, nw_ref, o_ref, xbuf, state):
    q = CHUNK
    lb = x_ref.shape[0]
    gw = GROUP_W
    ns = SSM_STATE

    @pl.when(pl.program_id(2) == 0)
    def _():
        xbuf[0:SUBLANE, :] = jnp.zeros((SUBLANE, gw + 2 * ns), F32)
        state[...] = jnp.zeros_like(state)

    xbuf[SUBLANE:SUBLANE + lb, 0:gw] = x_ref[...]
    xbuf[SUBLANE:SUBLANE + lb, gw:gw + ns] = b_ref[...]
    xbuf[SUBLANE:SUBLANE + lb, gw + ns:gw + 2 * ns] = c_ref[...]
    w = jnp.concatenate([wx_ref[...], wb_ref[...], wc_ref[...]], axis=1)
    conv = jnp.concatenate([bx_ref[...], bb_ref[...], bc_ref[...]], axis=1)
    for k in range(SSM_CONV):
        off = SUBLANE - (SSM_CONV - 1) + k
        conv = conv + w[k:k + 1, :] * xbuf[off:off + lb, :]
    xbuf[0:SUBLANE, :] = xbuf[lb:lb + SUBLANE, :]
    xc = conv * _sigmoid(conv)
    xs = xc[:, 0:gw]
    bm = xc[:, gw:gw + ns]
    cm = xc[:, gw + ns:gw + 2 * ns]

    dt_raw = dt_ref[...] + dtb_ref[...]
    dt = jnp.maximum(dt_raw, 0.0) + jnp.log(1.0 + jnp.exp(-jnp.abs(dt_raw)))
    a_neg = -jnp.exp(alog_ref[...])
    e = e_ref[0]
    dt_e = _dot_f32(dt, e)
    a_e = _dot_f32(dt * a_neg, e)
    d_e = _dot_f32(jnp.broadcast_to(dsk_ref[...], (SUBLANE, LANE)), e)[0:1]
    nw = nw_ref[...]

    li = lax.broadcasted_iota(jnp.int32, (q, gw), 0)
    si = lax.broadcasted_iota(jnp.int32, (q, gw), 1) & (q - 1)
    diag = li == si
    causal = li >= si
    tri = (lax.broadcasted_iota(jnp.int32, (q, q), 0) >= lax.broadcasted_iota(jnp.int32, (q, q), 1)).astype(F32)
    br = lax.broadcasted_iota(jnp.int32, (gw, gw), 0) // SSM_HEAD_DIM
    bc = lax.broadcasted_iota(jnp.int32, (gw, gw), 1) // SSM_HEAD_DIM
    blockdiag = br == bc

    st = state[...]
    for c in range(lb // q):
        r0 = c * q
        xs_c = xs[r0:r0 + q]
        bm_c = bm[r0:r0 + q].astype(BF16)
        cm_c = cm[r0:r0 + q].astype(BF16)
        acs = _dot_f32(tri, a_e[r0:r0 + q])
        last = acs[q - 1:q, :]
        arow = jnp.sum(jnp.where(diag, acs, 0.0), axis=0, keepdims=True)
        decay = jnp.exp(jnp.where(causal, acs - arow, -jnp.inf))
        b4 = jnp.concatenate([bm_c] * SSM_HPG, axis=0)
        cb = lax.dot_general(cm_c, b4, (((1,), (1,)), ((), ())), preferred_element_type=F32)
        m = (cb * decay).astype(BF16)
        xd = xs_c * dt_e[r0:r0 + q]
        xd_b = xd.astype(BF16)
        xbd = jnp.where(blockdiag, jnp.concatenate([xd_b] * SSM_HPG, axis=0), jnp.zeros((), BF16))
        y = _dot(m, xbd)
        y = y + _dot(cm_c, st.astype(BF16)) * jnp.exp(acs)
        xds = (xd * jnp.exp(last - acs)).astype(BF16)
        local = lax.dot_general(bm_c, xds, (((0,), (0,)), ((), ())), preferred_element_type=F32)
        st = jnp.exp(last) * st + local
        y = y + d_e * xs_c
        z_c = z_ref[r0:r0 + q, :]
        gz = y * (z_c * _sigmoid(z_c))
        ms = jnp.mean(gz * gz, axis=-1, keepdims=True)
        o_ref[r0:r0 + q, :] = (gz * lax.rsqrt(ms + EPS) * nw).astype(o_ref.dtype)
    state[...] = st


def _ssd(zx, dt, conv_w, conv_b, dt_bias, a_log, d_skip, norm_w, seq):
    t = zx.shape[0]
    gw, ns, g = GROUP_W, SSM_STATE, SSM_GROUPS
    d_inner = g * gw
    lb = min(512, seq)
    nj = seq // lb
    bsz = t // seq
    heads = dt_bias.shape[0]

    def pad_lanes(v):
        return jnp.zeros((1, LANE), F32).at[0, :heads].set(v)

    wpad = jnp.zeros((SUBLANE, conv_w.shape[1]), F32).at[:SSM_CONV].set(conv_w)
    cb2 = conv_b.reshape(1, -1)
    hh = jnp.arange(LANE)[None, :, None]
    cc = jnp.arange(gw)[None, None, :] // SSM_HEAD_DIM
    gg = jnp.arange(g)[:, None, None]
    expand = (hh == gg * SSM_HPG + cc).astype(F32)

    row = lambda b, gi, j: b * nj + j
    xoff, boff, coff = d_inner // gw, (2 * d_inner) // ns, (2 * d_inner + g * ns) // ns
    wb_off, wc_off = d_inner // ns, (d_inner + g * ns) // ns
    in_specs = [
        pl.BlockSpec((lb, gw), lambda b, gi, j: (row(b, gi, j), gi)),
        pl.BlockSpec((lb, gw), lambda b, gi, j: (row(b, gi, j), xoff + gi)),
        pl.BlockSpec((lb, ns), lambda b, gi, j: (row(b, gi, j), boff + gi)),
        pl.BlockSpec((lb, ns), lambda b, gi, j: (row(b, gi, j), coff + gi)),
        pl.BlockSpec((lb, LANE), lambda b, gi, j: (row(b, gi, j), 0)),
        pl.BlockSpec((SUBLANE, gw), lambda b, gi, j: (0, gi)),
        pl.BlockSpec((SUBLANE, ns), lambda b, gi, j: (0, wb_off + gi)),
        pl.BlockSpec((SUBLANE, ns), lambda b, gi, j: (0, wc_off + gi)),
        pl.BlockSpec((1, gw), lambda b, gi, j: (0, gi)),
        pl.BlockSpec((1, ns), lambda b, gi, j: (0, wb_off + gi)),
        pl.BlockSpec((1, ns), lambda b, gi, j: (0, wc_off + gi)),
        pl.BlockSpec((1, LANE), lambda b, gi, j: (0, 0)),
        pl.BlockSpec((1, LANE), lambda b, gi, j: (0, 0)),
        pl.BlockSpec((1, LANE), lambda b, gi, j: (0, 0)),
        pl.BlockSpec((1, LANE, gw), lambda b, gi, j: (gi, 0, 0)),
        pl.BlockSpec((1, gw), lambda b, gi, j: (0, gi)),
    ]
    return pl.pallas_call(
        _ssd_kernel,
        grid=(bsz, g, nj),
        in_specs=in_specs,
        out_specs=pl.BlockSpec((lb, gw), lambda b, gi, j: (row(b, gi, j), gi)),
        out_shape=jax.ShapeDtypeStruct((t, d_inner), BF16),
        scratch_shapes=[pltpu.VMEM((lb + SUBLANE, gw + 2 * ns), F32), pltpu.VMEM((ns, gw), F32)],
        compiler_params=_cparams(("parallel", "parallel", "arbitrary")),
        name="ssd",
    )(zx, zx, zx, zx, dt, wpad, wpad, wpad, cb2, cb2, cb2,
      pad_lanes(dt_bias), pad_lanes(a_log), pad_lanes(d_skip), expand, norm_w.reshape(1, -1))


def _attn_kernel(q_ref, k_ref, v_ref, lamp_ref, subw_ref, o_ref, vt_ref, acc_ref, m_ref, l_ref,
                 *, lambda_init):
    tq = q_ref.shape[1]
    tk = vt_ref.shape[2]
    seq = k_ref.shape[1]
    dh = DIFF_HEAD_DIM
    qi = pl.program_id(2)

    @pl.when(qi == 0)
    def _():
        for i in range(seq // tk):
            vt_ref[i] = v_ref[0, i * tk:(i + 1) * tk, :].astype(F32).T.astype(BF16)

    qb = q_ref[0] * jnp.asarray(dh ** -0.5, BF16)
    lane = lax.broadcasted_iota(jnp.int32, qb.shape, 1)
    zero = jnp.zeros((), BF16)
    qs = (jnp.where(lane < dh, qb, zero), jnp.where(lane >= dh, qb, zero))

    m_ref[...] = jnp.full(m_ref.shape, -jnp.inf, F32)
    l_ref[...] = jnp.zeros(l_ref.shape, F32)
    acc_ref[...] = jnp.zeros(acc_ref.shape, F32)

    def block(kj, mask):
        r0 = pl.multiple_of(kj * tk, tk)
        kb = k_ref[0, pl.ds(r0, tk), :]
        vtb = vt_ref[kj]
        for i in range(2):
            s = lax.dot_general(kb, qs[i], (((1,), (1,)), ((), ())), preferred_element_type=F32)
            if mask is not None:
                s = jnp.where(mask, s, -jnp.inf)
            m_old = m_ref[i]
            m_new = jnp.maximum(m_old, jnp.max(s, axis=0, keepdims=True))
            alpha = jnp.exp(m_old - m_new)
            p = jnp.exp(s - m_new)
            l_ref[i] = alpha * l_ref[i] + jnp.sum(p, axis=0, keepdims=True)
            acc_ref[i] = alpha * acc_ref[i] + _dot(vtb, p.astype(BF16))
            m_ref[i] = m_new

    def body(kj, carry):
        block(kj, None)
        return carry

    lax.fori_loop(0, qi * (tq // tk), body, 0)
    kc = lax.broadcasted_iota(jnp.int32, (tk, tq), 0) // CHUNK
    qc = lax.broadcasted_iota(jnp.int32, (tk, tq), 1) // CHUNK
    block(qi * (tq // tk), kc <= qc)

    lp = lamp_ref[...]
    lam = (jnp.exp(jnp.sum(lp[0:1] * lp[1:2], axis=-1, keepdims=True))
           - jnp.exp(jnp.sum(lp[2:3] * lp[3:4], axis=-1, keepdims=True)) + lambda_init)
    o = acc_ref[0] / l_ref[0] - lam * (acc_ref[1] / l_ref[1])
    ms = jnp.mean(o * o, axis=0, keepdims=True)
    sw = jnp.concatenate([subw_ref[...]] * (tq // LANE), axis=1)
    o = o * lax.rsqrt(ms + EPS) * sw * (1.0 - lambda_init)
    o_ref[0] = o.T.astype(o_ref.dtype)


def _attn(qkv, lq1, lk1, lq2, lk2, subln_w, lambda_init, bsz, seq):
    d3 = qkv.shape[1]
    d = d3 // 3
    heads = d // (2 * DIFF_HEAD_DIM)
    tq = tk = min(256, seq)
    assert tq == tk and tq % CHUNK == 0
    qkv3 = qkv.reshape(bsz, seq, d3)
    lamp = jnp.zeros((SUBLANE, LANE), F32)
    for r, v in enumerate((lq1, lk1, lq2, lk2)):
        lamp = lamp.at[r, :v.shape[0]].set(v)
    subw = jnp.broadcast_to(subln_w[:, None], (DIFF_V_DIM, LANE)).astype(F32)
    out = pl.pallas_call(
        functools.partial(_attn_kernel, lambda_init=lambda_init),
        grid=(bsz, heads, seq // tq),
        in_specs=[
            pl.BlockSpec((1, tq, LANE), lambda b, h, i: (b, i, h)),
            pl.BlockSpec((1, seq, LANE), lambda b, h, i: (b, 0, heads + h)),
            pl.BlockSpec((1, seq, LANE), lambda b, h, i: (b, 0, 2 * heads + h)),
            pl.BlockSpec((SUBLANE, LANE), lambda b, h, i: (0, 0)),
            pl.BlockSpec((DIFF_V_DIM, LANE), lambda b, h, i: (0, 0)),
        ],
        out_specs=pl.BlockSpec((1, tq, LANE), lambda b, h, i: (b, i, h)),
        out_shape=jax.ShapeDtypeStruct((bsz, seq, d), BF16),
        scratch_shapes=[
            pltpu.VMEM((seq // tk, DIFF_V_DIM, tk), BF16),
            pltpu.VMEM((2, DIFF_V_DIM, tq), F32),
            pltpu.VMEM((2, 1, tq), F32),
            pltpu.VMEM((2, 1, tq), F32),
        ],
        compiler_params=_cparams(("parallel", "parallel", "arbitrary")),
        name="attn",
    )(qkv3, qkv3, qkv3, lamp, subw)
    return out.reshape(bsz * seq, d)


def kernel(x, c, ada_w, ada_b, ln1_g, ln1_b, ln2_g, ln2_b, mlp_w1, mlp_w2, ssm_w_in, ssm_conv_w, ssm_conv_b, ssm_dt_bias, ssm_a_log, ssm_d, ssm_norm_w, ssm_w_out, attn_w_qkv, attn_lq1, attn_lk1, attn_lq2, attn_lk2, attn_subln_w, attn_w_out):
    bsz, seq, d = x.shape
    depth = ada_w.shape[0]
    mods = _mods(c, ada_w, ada_b)
    x2 = x.reshape(bsz * seq, d)
    n_main = SSM_GROUPS * GROUP_W * 2 + 2 * SSM_GROUPS * SSM_STATE
    for i in range(depth):
        j = i // 2
        if i % 2 == 0:
            w_in = ssm_w_in[j]
            heads = w_in.shape[1] - n_main
            w_dt = jnp.zeros((d, LANE), F32).at[:, :heads].set(w_in[:, n_main:])
            zx, dt = _proj(x2, mods[i], 1, 0, w_in[:, :n_main].astype(BF16), F32, seq, w_dt=w_dt.astype(BF16))
            a = _ssd(zx, dt, ssm_conv_w[j], ssm_conv_b[j], ssm_dt_bias[j], ssm_a_log[j], ssm_d[j],
                     ssm_norm_w[j], seq)
            w_out = ssm_w_out[j]
        else:
            lambda_init = 0.8 - 0.6 * math.exp(-0.3 * i)
            qkv = _proj(x2, mods[i], 1, 0, attn_w_qkv[j].astype(BF16), BF16, seq)
            a = _attn(qkv, attn_lq1[j], attn_lk1[j], attn_lq2[j], attn_lk2[j], attn_subln_w[j],
                      lambda_init, bsz, seq)
            w_out = attn_w_out[j]
        x2 = _outproj(a, w_out.astype(BF16), x2, mods[i], 2, ln1_g[i], ln1_b[i], seq)
        x2 = _mlp(x2, mods[i], mlp_w1[i].astype(BF16), mlp_w2[i].astype(BF16), ln2_g[i], ln2_b[i], seq)
    return x2.reshape(bsz, seq, d)
```

```python
import functools
import math

import jax
import jax.numpy as jnp
from jax import lax
from jax.experimental import pallas as pl
from jax.experimental.pallas import tpu as pltpu

F32 = jnp.float32
BF16 = jnp.bfloat16

DEPTH = 2
CHUNK = 64
SSM_HEAD_DIM = 64
SSM_GROUPS = 8
SSM_HPG = 4
SSM_STATE = 128
SSM_CONV = 4
GROUP_W = SSM_HPG * SSM_HEAD_DIM
DIFF_HEAD_DIM = 64
DIFF_V_DIM = 128
ATTN_ONES_ROWS = 16
DEEPNORM_ALPHA = (2 * DEPTH) ** 0.25
EPS = 1e-5

LANE = 128
SUBLANE = 8
VMEM_LIMIT = 56 * 1024 * 1024


def _cparams(sem):
    return pltpu.CompilerParams(dimension_semantics=sem, vmem_limit_bytes=VMEM_LIMIT)


def _sigmoid(x):
    return 1.0 / (1.0 + jnp.exp(-x))


def _layer_norm(v, g, b):
    mu = jnp.mean(v, axis=-1, keepdims=True)
    vc = v - mu
    var = jnp.mean(vc * vc, axis=-1, keepdims=True)
    return vc * lax.rsqrt(var + EPS) * g + b


def _dot(a, b):
    return jnp.dot(a, b, preferred_element_type=F32)


def _dot_f32(a, b):
    return jnp.dot(a, b, preferred_element_type=F32, precision=lax.Precision.HIGHEST)


def _mods_kernel(c_ref, w_ref, b_ref, o_ref):
    c = c_ref[...]
    cond = c * _sigmoid(c)
    o_ref[0] = _dot(cond.astype(BF16), w_ref[0].astype(BF16)) + b_ref[0]


def _mods(c, ada_w, ada_b):
    depth, d, n = ada_w.shape
    bsz = c.shape[0]
    cp = jnp.zeros((SUBLANE, d), F32).at[:bsz].set(c)
    tn = 1024
    out = pl.pallas_call(
        _mods_kernel,
        grid=(depth, n // tn),
        in_specs=[
            pl.BlockSpec((SUBLANE, d), lambda l, j: (0, 0)),
            pl.BlockSpec((1, d, tn), lambda l, j: (l, 0, j)),
            pl.BlockSpec((1, 1, tn), lambda l, j: (l, 0, j)),
        ],
        out_specs=pl.BlockSpec((1, SUBLANE, tn), lambda l, j: (l, 0, j)),
        out_shape=jax.ShapeDtypeStruct((depth, SUBLANE, n), F32),
        compiler_params=_cparams(("parallel", "parallel")),
        name="mods",
    )(cp, ada_w, ada_b.reshape(depth, 1, n))
    return out[:, :bsz].reshape(depth, bsz * 6, 1, d)


def _proj_kernel(x_ref, sc_ref, sh_ref, w_ref, cs_ref, o_ref, hb_ref):
    @pl.when(pl.program_id(1) == 0)
    def _():
        h = x_ref[...] * (1.0 + sc_ref[0]) + sh_ref[0]
        hb_ref[...] = h.astype(BF16)

    o_ref[...] = (_dot(hb_ref[...], w_ref[...]) * cs_ref[...]).astype(o_ref.dtype)


def _proj_dt_kernel(x_ref, sc_ref, sh_ref, w_ref, wdt_ref, o_ref, dt_ref, hb_ref):
    @pl.when(pl.program_id(1) == 0)
    def _():
        h = x_ref[...] * (1.0 + sc_ref[0]) + sh_ref[0]
        hb_ref[...] = h.astype(BF16)
        dt_ref[...] = _dot(hb_ref[...], wdt_ref[...])

    o_ref[...] = _dot(hb_ref[...], w_ref[...]).astype(o_ref.dtype)


def _proj(x2, mods, k_scale, k_shift, w, out_dtype, seq, w_dt=None, col_scale=None):
    t, d = x2.shape
    n = w.shape[1]
    tm = min(1024, seq)
    tn = min(1024, n)
    per_seq = seq // tm
    in_specs = [
        pl.BlockSpec((tm, d), lambda i, j: (i, 0)),
        pl.BlockSpec((1, 1, d), lambda i, j: ((i // per_seq) * 6 + k_scale, 0, 0)),
        pl.BlockSpec((1, 1, d), lambda i, j: ((i // per_seq) * 6 + k_shift, 0, 0)),
        pl.BlockSpec((d, tn), lambda i, j: (0, j)),
    ]
    scratch = [pltpu.VMEM((tm, d), BF16)]
    cp = _cparams(("parallel", "arbitrary"))
    if w_dt is None:
        return pl.pallas_call(
            _proj_kernel,
            grid=(t // tm, n // tn),
            in_specs=in_specs + [pl.BlockSpec((1, tn), lambda i, j: (0, j))],
            out_specs=pl.BlockSpec((tm, tn), lambda i, j: (i, j)),
            out_shape=jax.ShapeDtypeStruct((t, n), out_dtype),
            scratch_shapes=scratch,
            compiler_params=cp,
            name="proj",
        )(x2, mods, mods, w, col_scale.reshape(1, n))
    ndt = w_dt.shape[1]
    return pl.pallas_call(
        _proj_dt_kernel,
        grid=(t // tm, n // tn),
        in_specs=in_specs + [pl.BlockSpec((d, ndt), lambda i, j: (0, 0))],
        out_specs=[
            pl.BlockSpec((tm, tn), lambda i, j: (i, j)),
            pl.BlockSpec((tm, ndt), lambda i, j: (i, 0)),
        ],
        out_shape=[
            jax.ShapeDtypeStruct((t, n), out_dtype),
            jax.ShapeDtypeStruct((t, ndt), F32),
        ],
        scratch_shapes=scratch,
        compiler_params=cp,
        name="proj_dt",
    )(x2, mods, mods, w, w_dt)


def _outproj_kernel(a_ref, w_ref, x_ref, g_ref, lng_ref, lnb_ref, o_ref):
    y = _dot(a_ref[...], w_ref[...])
    v = DEEPNORM_ALPHA * x_ref[...] + g_ref[0] * y
    o_ref[...] = _layer_norm(v, lng_ref[...], lnb_ref[...])


def _outproj(a, w, x2, mods, k_gate, ln_g, ln_b, seq):
    t, d = x2.shape
    k = a.shape[1]
    tm = min(512, seq)
    per_seq = seq // tm
    return pl.pallas_call(
        _outproj_kernel,
        grid=(t // tm,),
        in_specs=[
            pl.BlockSpec((tm, k), lambda i: (i, 0)),
            pl.BlockSpec((k, d), lambda i: (0, 0)),
            pl.BlockSpec((tm, d), lambda i: (i, 0)),
            pl.BlockSpec((1, 1, d), lambda i: ((i // per_seq) * 6 + k_gate, 0, 0)),
            pl.BlockSpec((1, d), lambda i: (0, 0)),
            pl.BlockSpec((1, d), lambda i: (0, 0)),
        ],
        out_specs=pl.BlockSpec((tm, d), lambda i: (i, 0)),
        out_shape=jax.ShapeDtypeStruct((t, d), F32),
        compiler_params=_cparams(("parallel",)),
        name="outproj",
    )(a, w, x2, mods, ln_g.reshape(1, d), ln_b.reshape(1, d))


def _mlp_kernel(x_ref, sc_ref, sh_ref, g_ref, w1_ref, w2_ref, lng_ref, lnb_ref, o_ref, hb_ref, acc_ref):
    f = pl.program_id(1)

    @pl.when(f == 0)
    def _():
        h = x_ref[...] * (1.0 + sc_ref[0]) + sh_ref[0]
        hb_ref[...] = h.astype(BF16)
        acc_ref[...] = jnp.zeros_like(acc_ref)

    u = jnp.maximum(_dot(hb_ref[...], w1_ref[...]), 0.0)
    acc_ref[...] += _dot((u * u).astype(BF16), w2_ref[...])

    @pl.when(f == pl.num_programs(1) - 1)
    def _():
        v = DEEPNORM_ALPHA * x_ref[...] + g_ref[0] * acc_ref[...]
        o_ref[...] = _layer_norm(v, lng_ref[...], lnb_ref[...])


def _mlp(x2, mods, w1, w2, ln_g, ln_b, seq):
    t, d = x2.shape
    ff = w1.shape[1]
    tm = min(1024, seq)
    tf = min(1024, ff)
    per_seq = seq // tm

    def mod_spec(k):
        return pl.BlockSpec((1, 1, d), lambda i, f: ((i // per_seq) * 6 + k, 0, 0))

    return pl.pallas_call(
        _mlp_kernel,
        grid=(t // tm, ff // tf),
        in_specs=[
            pl.BlockSpec((tm, d), lambda i, f: (i, 0)),
            mod_spec(4), mod_spec(3), mod_spec(5),
            pl.BlockSpec((d, tf), lambda i, f: (0, f)),
            pl.BlockSpec((tf, d), lambda i, f: (f, 0)),
            pl.BlockSpec((1, d), lambda i, f: (0, 0)),
            pl.BlockSpec((1, d), lambda i, f: (0, 0)),
        ],
        out_specs=pl.BlockSpec((tm, d), lambda i, f: (i, 0)),
        out_shape=jax.ShapeDtypeStruct((t, d), F32),
        scratch_shapes=[pltpu.VMEM((tm, d), BF16), pltpu.VMEM((tm, d), F32)],
        compiler_params=_cparams(("parallel", "arbitrary")),
        name="mlp",
    )(x2, mods, mods, mods, w1, w2, ln_g.reshape(1, d), ln_b.reshape(1, d))


def _ssd_kernel(z_ref, x_ref, b_ref, c_ref, dt_ref, wx_ref, wb_ref, wc_ref, bx_ref, bb_ref, bc_ref,
                dtb_ref, alog_ref, dsk_ref, e_ref, nw_ref, o_ref, xbuf, state):
    q = CHUNK
    lb = x_ref.shape[0]
    gw = GROUP_W
    ns = SSM_STATE

    @pl.when(pl.program_id(2) == 0)
    def _():
        xbuf[0:SUBLANE, :] = jnp.zeros((SUBLANE, gw + 2 * ns), F32)
        state[...] = jnp.zeros_like(state)

    xbuf[SUBLANE:SUBLANE + lb, 0:gw] = x_ref[...]
    xbuf[SUBLANE:SUBLANE + lb, gw:gw + ns] = b_ref[...]
    xbuf[SUBLANE:SUBLANE + lb, gw + ns:gw + 2 * ns] = c_ref[...]
    w = jnp.concatenate([wx_ref[...], wb_ref[...], wc_ref[...]], axis=1)
    conv = jnp.concatenate([bx_ref[...], bb_ref[...], bc_ref[...]], axis=1)
    for k in range(SSM_CONV):
        off = SUBLANE - (SSM_CONV - 1) + k
        conv = conv + w[k:k + 1, :] * xbuf[off:off + lb, :]
    xbuf[0:SUBLANE, :] = xbuf[lb:lb + SUBLANE, :]
    xc = conv * _sigmoid(conv)
    xs = xc[:, 0:gw]
    bm = xc[:, gw:gw + ns]
    cm = xc[:, gw + ns:gw + 2 * ns]

    dt_raw = dt_ref[...] + dtb_ref[...]
    dt = jnp.maximum(dt_raw, 0.0) + jnp.log(1.0 + jnp.exp(-jnp.abs(dt_raw)))
    a_neg = -jnp.exp(alog_ref[...])
    e = e_ref[0]
    dt_e = _dot_f32(dt, e)
    a_e = _dot_f32(dt * a_neg, e)
    d_e = _dot_f32(jnp.broadcast_to(dsk_ref[...], (SUBLANE, LANE)), e)[0:1]
    nw = nw_ref[...]

    li = lax.broadcasted_iota(jnp.int32, (q, gw), 0)
    si = lax.broadcasted_iota(jnp.int32, (q, gw), 1) & (q - 1)
    diag = li == si
    causal = li >= si
    tri = (lax.broadcasted_iota(jnp.int32, (q, q), 0) >= lax.broadcasted_iota(jnp.int32, (q, q), 1)).astype(F32)
    br = lax.broadcasted_iota(jnp.int32, (gw, gw), 0) // SSM_HEAD_DIM
    bc = lax.broadcasted_iota(jnp.int32, (gw, gw), 1) // SSM_HEAD_DIM
    blockdiag = br == bc

    st = state[...]
    for c in range(lb // q):
        r0 = c * q
        xs_c = xs[r0:r0 + q]
        bm_c = bm[r0:r0 + q].astype(BF16)
        cm_c = cm[r0:r0 + q].astype(BF16)
        acs = _dot_f32(tri, a_e[r0:r0 + q])
        last = acs[q - 1:q, :]
        arow = jnp.sum(jnp.where(diag, acs, 0.0), axis=0, keepdims=True)
        decay = jnp.exp(jnp.where(causal, acs - arow, -jnp.inf))
        b4 = jnp.concatenate([bm_c] * SSM_HPG, axis=0)
        cb = lax.dot_general(cm_c, b4, (((1,), (1,)), ((), ())), preferred_element_type=F32)
        m = (cb * decay).astype(BF16)
        xd = xs_c * dt_e[r0:r0 + q]
        xd_b = xd.astype(BF16)
        xbd = jnp.where(blockdiag, jnp.concatenate([xd_b] * SSM_HPG, axis=0), jnp.zeros((), BF16))
        y = _dot(m, xbd)
        y = y + _dot(cm_c, st.astype(BF16)) * jnp.exp(acs)
        xds = (xd * jnp.exp(last - acs)).astype(BF16)
        local = lax.dot_general(bm_c, xds, (((0,), (0,)), ((), ())), preferred_element_type=F32)
        st = jnp.exp(last) * st + local
        y = y + d_e * xs_c
        z_c = z_ref[r0:r0 + q, :]
        gz = y * (z_c * _sigmoid(z_c))
        ms = jnp.mean(gz * gz, axis=-1, keepdims=True)
        o_ref[r0:r0 + q, :] = (gz * lax.rsqrt(ms + EPS) * nw).astype(o_ref.dtype)
    state[...] = st


def _ssd(zx, dt, conv_w, conv_b, dt_bias, a_log, d_skip, norm_w, seq):
    t = zx.shape[0]
    gw, ns, g = GROUP_W, SSM_STATE, SSM_GROUPS
    d_inner = g * gw
    lb = min(512, seq)
    nj = seq // lb
    bsz = t // seq
    heads = dt_bias.shape[0]

    def pad_lanes(v):
        return jnp.zeros((1, LANE), F32).at[0, :heads].set(v)

    wpad = jnp.zeros((SUBLANE, conv_w.shape[1]), F32).at[:SSM_CONV].set(conv_w)
    cb2 = conv_b.reshape(1, -1)
    hh = jnp.arange(LANE)[None, :, None]
    cc = jnp.arange(gw)[None, None, :] // SSM_HEAD_DIM
    gg = jnp.arange(g)[:, None, None]
    expand = (hh == gg * SSM_HPG + cc).astype(F32)

    row = lambda b, gi, j: b * nj + j
    xoff, boff, coff = d_inner // gw, (2 * d_inner) // ns, (2 * d_inner + g * ns) // ns
    wb_off, wc_off = d_inner // ns, (d_inner + g * ns) // ns
    in_specs = [
        pl.BlockSpec((lb, gw), lambda b, gi, j: (row(b, gi, j), gi)),
        pl.BlockSpec((lb, gw), lambda b, gi, j: (row(b, gi, j), xoff + gi)),
        pl.BlockSpec((lb, ns), lambda b, gi, j: (row(b, gi, j), boff + gi)),
        pl.BlockSpec((lb, ns), lambda b, gi, j: (row(b, gi, j), coff + gi)),
        pl.BlockSpec((lb, LANE), lambda b, gi, j: (row(b, gi, j), 0)),
        pl.BlockSpec((SUBLANE, gw), lambda b, gi, j: (0, gi)),
        pl.BlockSpec((SUBLANE, ns), lambda b, gi, j: (0, wb_off + gi)),
        pl.BlockSpec((SUBLANE, ns), lambda b, gi, j: (0, wc_off + gi)),
        pl.BlockSpec((1, gw), lambda b, gi, j: (0, gi)),
        pl.BlockSpec((1, ns), lambda b, gi, j: (0, wb_off + gi)),
        pl.BlockSpec((1, ns), lambda b, gi, j: (0, wc_off + gi)),
        pl.BlockSpec((1, LANE), lambda b, gi, j: (0, 0)),
        pl.BlockSpec((1, LANE), lambda b, gi, j: (0, 0)),
        pl.BlockSpec((1, LANE), lambda b, gi, j: (0, 0)),
        pl.BlockSpec((1, LANE, gw), lambda b, gi, j: (gi, 0, 0)),
        pl.BlockSpec((1, gw), lambda b, gi, j: (0, gi)),
    ]
    return pl.pallas_call(
        _ssd_kernel,
        grid=(bsz, g, nj),
        in_specs=in_specs,
        out_specs=pl.BlockSpec((lb, gw), lambda b, gi, j: (row(b, gi, j), gi)),
        out_shape=jax.ShapeDtypeStruct((t, d_inner), BF16),
        scratch_shapes=[pltpu.VMEM((lb + SUBLANE, gw + 2 * ns), F32), pltpu.VMEM((ns, gw), F32)],
        compiler_params=_cparams(("parallel", "parallel", "arbitrary")),
        name="ssd",
    )(zx, zx, zx, zx, dt, wpad, wpad, wpad, cb2, cb2, cb2,
      pad_lanes(dt_bias), pad_lanes(a_log), pad_lanes(d_skip), expand, norm_w.reshape(1, -1))


def _attn_kernel(q_ref, k_ref, v_ref, lamp_ref, subw_ref, o_ref, vt_ref, acc_ref, m_ref,
                 sa_ref, bma_ref, sb_ref, bmb_ref,
                 *, lambda_init):
    tq = q_ref.shape[1]
    tk = vt_ref.shape[2]
    seq = k_ref.shape[1]
    dh = DIFF_HEAD_DIM
    dv = DIFF_V_DIM
    qi = pl.program_id(2)

    @pl.when(qi == 0)
    def _():
        for i in range(seq // tk):
            vt_ref[i, 0:dv, :] = v_ref[0, i * tk:(i + 1) * tk, :].astype(F32).T.astype(BF16)
            vt_ref[i, dv:, :] = jnp.ones((vt_ref.shape[1] - dv, tk), BF16)

    qb = q_ref[0]
    lane = lax.broadcasted_iota(jnp.int32, qb.shape, 1)
    zero = jnp.zeros((), BF16)
    qs = (jnp.where(lane < dh, qb, zero), jnp.where(lane >= dh, qb, zero))

    m_ref[...] = jnp.full(m_ref.shape, -jnp.inf, F32)
    acc_ref[...] = jnp.zeros(acc_ref.shape, F32)

    def scores(kj, s_ref, bm_ref, masked):
        r0 = pl.multiple_of(kj * tk, tk)
        kb = k_ref[0, pl.ds(r0, tk), :]
        for i in range(2):
            s = lax.dot_general(kb, qs[i], (((1,), (1,)), ((), ())), preferred_element_type=F32)
            if masked:
                kc = lax.broadcasted_iota(jnp.int32, (tk, tq), 0) // CHUNK
                qc = lax.broadcasted_iota(jnp.int32, (tk, tq), 1) // CHUNK
                s = jnp.where(kc <= qc, s, -jnp.inf)
            s_ref[i] = s
            bm_ref[i] = jnp.max(s, axis=0, keepdims=True)

    def update(kj, s_ref, bm_ref):
        vtb = vt_ref[kj]
        for i in range(2):
            m_old = m_ref[i]
            m_new = jnp.maximum(m_old, bm_ref[i])
            alpha = jnp.exp2(m_old - m_new)
            p = jnp.exp2(s_ref[i] - m_new)
            acc_ref[i] = alpha * acc_ref[i] + _dot(vtb, p.astype(BF16))
            m_ref[i] = m_new

    a_buf, b_buf = (sa_ref, bma_ref), (sb_ref, bmb_ref)

    @pl.when(qi > 0)
    def _():
        scores(0, *a_buf, False)

    def body(u, carry):
        scores(2 * u + 1, *b_buf, False)
        update(2 * u, *a_buf)
        scores(2 * u + 2, *a_buf, False)
        update(2 * u + 1, *b_buf)
        return carry

    lax.fori_loop(0, jnp.maximum(qi - 1, 0) // 2, body, 0)

    @pl.when(qi == 0)
    def _():
        scores(0, *a_buf, True)
        update(0, *a_buf)

    @pl.when(qi % 2 == 1)
    def _():
        scores(qi, *b_buf, True)
        update(qi - 1, *a_buf)
        update(qi, *b_buf)

    @pl.when(jnp.logical_and(qi % 2 == 0, qi > 0))
    def _():
        scores(qi - 1, *b_buf, False)
        update(qi - 2, *a_buf)
        scores(qi, *a_buf, True)
        update(qi - 1, *b_buf)
        update(qi, *a_buf)

    lp = lamp_ref[...]
    lam = (jnp.exp(jnp.sum(lp[0:1] * lp[1:2], axis=-1, keepdims=True))
           - jnp.exp(jnp.sum(lp[2:3] * lp[3:4], axis=-1, keepdims=True)) + lambda_init)
    o = (acc_ref[0, 0:dv, :] / acc_ref[0, dv:dv + 1, :]
         - lam * (acc_ref[1, 0:dv, :] / acc_ref[1, dv:dv + 1, :]))
    ms = jnp.mean(o * o, axis=0, keepdims=True)
    sw = jnp.concatenate([subw_ref[...]] * (tq // LANE), axis=1)
    o = o * lax.rsqrt(ms + EPS) * sw * (1.0 - lambda_init)
    o_ref[0] = o.T.astype(o_ref.dtype)


def _attn(qkv, lq1, lk1, lq2, lk2, subln_w, lambda_init, bsz, seq):
    d3 = qkv.shape[1]
    d = d3 // 3
    heads = d // (2 * DIFF_HEAD_DIM)
    tq = tk = min(512, seq)
    assert tq == tk and tq % CHUNK == 0
    qkv3 = qkv.reshape(bsz, seq, d3)
    lamp = jnp.zeros((SUBLANE, LANE), F32)
    for r, v in enumerate((lq1, lk1, lq2, lk2)):
        lamp = lamp.at[r, :v.shape[0]].set(v)
    subw = jnp.broadcast_to(subln_w[:, None], (DIFF_V_DIM, LANE)).astype(F32)
    out = pl.pallas_call(
        functools.partial(_attn_kernel, lambda_init=lambda_init),
        grid=(bsz, heads, seq // tq),
        in_specs=[
            pl.BlockSpec((1, tq, LANE), lambda b, h, i: (b, i, h)),
            pl.BlockSpec((1, seq, LANE), lambda b, h, i: (b, 0, heads + h)),
            pl.BlockSpec((1, seq, LANE), lambda b, h, i: (b, 0, 2 * heads + h)),
            pl.BlockSpec((SUBLANE, LANE), lambda b, h, i: (0, 0)),
            pl.BlockSpec((DIFF_V_DIM, LANE), lambda b, h, i: (0, 0)),
        ],
        out_specs=pl.BlockSpec((1, tq, LANE), lambda b, h, i: (b, i, h)),
        out_shape=jax.ShapeDtypeStruct((bsz, seq, d), BF16),
        scratch_shapes=[
            pltpu.VMEM((seq // tk, DIFF_V_DIM + ATTN_ONES_ROWS, tk), BF16),
            pltpu.VMEM((2, DIFF_V_DIM + ATTN_ONES_ROWS, tq), F32),
            pltpu.VMEM((2, 1, tq), F32),
            pltpu.VMEM((2, tk, tq), F32), pltpu.VMEM((2, 1, tq), F32),
            pltpu.VMEM((2, tk, tq), F32), pltpu.VMEM((2, 1, tq), F32),
        ],
        compiler_params=_cparams(("parallel", "parallel", "arbitrary")),
        name="attn",
    )(qkv3, qkv3, qkv3, lamp, subw)
    return out.reshape(bsz * seq, d)


def kernel(x, c, ada_w, ada_b, ln1_g, ln1_b, ln2_g, ln2_b, mlp_w1, mlp_w2, ssm_w_in, ssm_conv_w, ssm_conv_b, ssm_dt_bias, ssm_a_log, ssm_d, ssm_norm_w, ssm_w_out, attn_w_qkv, attn_lq1, attn_lk1, attn_lq2, attn_lk2, attn_subln_w, attn_w_out):
    bsz, seq, d = x.shape
    depth = ada_w.shape[0]
    mods = _mods(c, ada_w, ada_b)
    x2 = x.reshape(bsz * seq, d)
    n_main = SSM_GROUPS * GROUP_W * 2 + 2 * SSM_GROUPS * SSM_STATE
    for i in range(depth):
        j = i // 2
        if i % 2 == 0:
            w_in = ssm_w_in[j]
            heads = w_in.shape[1] - n_main
            w_dt = jnp.zeros((d, LANE), F32).at[:, :heads].set(w_in[:, n_main:])
            zx, dt = _proj(x2, mods[i], 1, 0, w_in[:, :n_main].astype(BF16), F32, seq, w_dt=w_dt.astype(BF16))
            a = _ssd(zx, dt, ssm_conv_w[j], ssm_conv_b[j], ssm_dt_bias[j], ssm_a_log[j], ssm_d[j],
                     ssm_norm_w[j], seq)
            w_out = ssm_w_out[j]
        else:
            lambda_init = 0.8 - 0.6 * math.exp(-0.3 * i)
            col_scale = jnp.concatenate([jnp.full((d,), math.log2(math.e) * DIFF_HEAD_DIM ** -0.5, F32),
                                         jnp.ones((2 * d,), F32)])
            qkv = _proj(x2, mods[i], 1, 0, attn_w_qkv[j].astype(BF16), BF16, seq, col_scale=col_scale)
            a = _attn(qkv, attn_lq1[j], attn_lk1[j], attn_lq2[j], attn_lk2[j], attn_subln_w[j],
                      lambda_init, bsz, seq)
            w_out = attn_w_out[j]
        x2 = _outproj(a, w_out.astype(BF16), x2, mods[i], 2, ln1_g[i], ln1_b[i], seq)
        x2 = _mlp(x2, mods[i], mlp_w1[i].astype(BF16), mlp_w2[i].astype(BF16), ln2_g[i], ln2_b[i], seq)
    return x2.reshape(bsz, seq, d)
```

```python
import functools
import math

import jax
import jax.numpy as jnp
from jax import lax
from jax.experimental import pallas as pl
from jax.experimental.pallas import tpu as pltpu

F32 = jnp.float32
BF16 = jnp.bfloat16

DEPTH = 2
CHUNK = 64
SSM_HEAD_DIM = 64
SSM_GROUPS = 8
SSM_HPG = 4
SSM_STATE = 128
SSM_CONV = 4
GROUP_W = SSM_HPG * SSM_HEAD_DIM
DIFF_HEAD_DIM = 64
DIFF_V_DIM = 128
ATTN_ONES_ROWS = 16
DEEPNORM_ALPHA = (2 * DEPTH) ** 0.25
EPS = 1e-5

LANE = 128
SUBLANE = 8
VMEM_LIMIT = 56 * 1024 * 1024


def _cparams(sem):
    return pltpu.CompilerParams(dimension_semantics=sem, vmem_limit_bytes=VMEM_LIMIT)


def _sigmoid(x):
    return 1.0 / (1.0 + jnp.exp(-x))


def _layer_norm(v, g, b):
    mu = jnp.mean(v, axis=-1, keepdims=True)
    vc = v - mu
    var = jnp.mean(vc * vc, axis=-1, keepdims=True)
    return vc * lax.rsqrt(var + EPS) * g + b


def _dot(a, b):
    return jnp.dot(a, b, preferred_element_type=F32)


def _mods_kernel(c_ref, w_ref, b_ref, o_ref):
    c = c_ref[...]
    cond = c * _sigmoid(c)
    o_ref[0] = _dot(cond.astype(BF16), w_ref[0].astype(BF16)) + b_ref[0]


def _mods(c, ada_w, ada_b):
    depth, d, n = ada_w.shape
    bsz = c.shape[0]
    cp = jnp.zeros((SUBLANE, d), F32).at[:bsz].set(c)
    tn = 1024
    out = pl.pallas_call(
        _mods_kernel,
        grid=(depth, n // tn),
        in_specs=[
            pl.BlockSpec((SUBLANE, d), lambda l, j: (0, 0)),
            pl.BlockSpec((1, d, tn), lambda l, j: (l, 0, j)),
            pl.BlockSpec((1, 1, tn), lambda l, j: (l, 0, j)),
        ],
        out_specs=pl.BlockSpec((1, SUBLANE, tn), lambda l, j: (l, 0, j)),
        out_shape=jax.ShapeDtypeStruct((depth, SUBLANE, n), F32),
        compiler_params=_cparams(("parallel", "parallel")),
        name="mods",
    )(cp, ada_w, ada_b.reshape(depth, 1, n))
    return out[:, :bsz].reshape(depth, bsz * 6, 1, d)


def _proj_kernel(x_ref, sc_ref, sh_ref, w_ref, cs_ref, o_ref, hb_ref):
    @pl.when(pl.program_id(1) == 0)
    def _():
        h = x_ref[...] * (1.0 + sc_ref[0]) + sh_ref[0]
        hb_ref[...] = h.astype(BF16)

    o_ref[...] = (_dot(hb_ref[...], w_ref[...]) * cs_ref[...]).astype(o_ref.dtype)


def _proj_dt_kernel(x_ref, sc_ref, sh_ref, w_ref, wdt_ref, o_ref, dt_ref, hb_ref):
    @pl.when(pl.program_id(1) == 0)
    def _():
        h = x_ref[...] * (1.0 + sc_ref[0]) + sh_ref[0]
        hb_ref[...] = h.astype(BF16)
        dt_ref[...] = _dot(hb_ref[...], wdt_ref[...])

    o_ref[...] = _dot(hb_ref[...], w_ref[...]).astype(o_ref.dtype)


def _proj(x2, mods, k_scale, k_shift, w, out_dtype, seq, w_dt=None, col_scale=None):
    t, d = x2.shape
    n = w.shape[1]
    tm = min(1024, seq)
    tn = min(1024, n)
    per_seq = seq // tm
    in_specs = [
        pl.BlockSpec((tm, d), lambda i, j: (i, 0)),
        pl.BlockSpec((1, 1, d), lambda i, j: ((i // per_seq) * 6 + k_scale, 0, 0)),
        pl.BlockSpec((1, 1, d), lambda i, j: ((i // per_seq) * 6 + k_shift, 0, 0)),
        pl.BlockSpec((d, tn), lambda i, j: (0, j)),
    ]
    scratch = [pltpu.VMEM((tm, d), BF16)]
    cp = _cparams(("parallel", "arbitrary"))
    if w_dt is None:
        return pl.pallas_call(
            _proj_kernel,
            grid=(t // tm, n // tn),
            in_specs=in_specs + [pl.BlockSpec((1, tn), lambda i, j: (0, j))],
            out_specs=pl.BlockSpec((tm, tn), lambda i, j: (i, j)),
            out_shape=jax.ShapeDtypeStruct((t, n), out_dtype),
            scratch_shapes=scratch,
            compiler_params=cp,
            name="proj",
        )(x2, mods, mods, w, col_scale.reshape(1, n))
    ndt = w_dt.shape[1]
    return pl.pallas_call(
        _proj_dt_kernel,
        grid=(t // tm, n // tn),
        in_specs=in_specs + [pl.BlockSpec((d, ndt), lambda i, j: (0, 0))],
        out_specs=[
            pl.BlockSpec((tm, tn), lambda i, j: (i, j)),
            pl.BlockSpec((tm, ndt), lambda i, j: (i, 0)),
        ],
        out_shape=[
            jax.ShapeDtypeStruct((t, n), out_dtype),
            jax.ShapeDtypeStruct((t, ndt), F32),
        ],
        scratch_shapes=scratch,
        compiler_params=cp,
        name="proj_dt",
    )(x2, mods, mods, w, w_dt)


def _outproj_kernel(a_ref, w_ref, x_ref, g_ref, lng_ref, lnb_ref, o_ref):
    y = _dot(a_ref[...], w_ref[...])
    v = DEEPNORM_ALPHA * x_ref[...] + g_ref[0] * y
    o_ref[...] = _layer_norm(v, lng_ref[...], lnb_ref[...])


def _outproj(a, w, x2, mods, k_gate, ln_g, ln_b, seq):
    t, d = x2.shape
    k = a.shape[1]
    tm = min(512, seq)
    per_seq = seq // tm
    return pl.pallas_call(
        _outproj_kernel,
        grid=(t // tm,),
        in_specs=[
            pl.BlockSpec((tm, k), lambda i: (i, 0)),
            pl.BlockSpec((k, d), lambda i: (0, 0)),
            pl.BlockSpec((tm, d), lambda i: (i, 0)),
            pl.BlockSpec((1, 1, d), lambda i: ((i // per_seq) * 6 + k_gate, 0, 0)),
            pl.BlockSpec((1, d), lambda i: (0, 0)),
            pl.BlockSpec((1, d), lambda i: (0, 0)),
        ],
        out_specs=pl.BlockSpec((tm, d), lambda i: (i, 0)),
        out_shape=jax.ShapeDtypeStruct((t, d), F32),
        compiler_params=_cparams(("parallel",)),
        name="outproj",
    )(a, w, x2, mods, ln_g.reshape(1, d), ln_b.reshape(1, d))


def _mlp_kernel(x_ref, sc_ref, sh_ref, g_ref, w1_ref, w2_ref, lng_ref, lnb_ref, o_ref, hb_ref, acc_ref):
    f = pl.program_id(1)

    @pl.when(f == 0)
    def _():
        h = x_ref[...] * (1.0 + sc_ref[0]) + sh_ref[0]
        hb_ref[...] = h.astype(BF16)
        acc_ref[...] = jnp.zeros_like(acc_ref)

    u = jnp.maximum(_dot(hb_ref[...], w1_ref[...]), 0.0)
    acc_ref[...] += _dot((u * u).astype(BF16), w2_ref[...])

    @pl.when(f == pl.num_programs(1) - 1)
    def _():
        v = DEEPNORM_ALPHA * x_ref[...] + g_ref[0] * acc_ref[...]
        o_ref[...] = _layer_norm(v, lng_ref[...], lnb_ref[...])


def _mlp(x2, mods, w1, w2, ln_g, ln_b, seq):
    t, d = x2.shape
    ff = w1.shape[1]
    tm = min(1024, seq)
    tf = min(1024, ff)
    per_seq = seq // tm

    def mod_spec(k):
        return pl.BlockSpec((1, 1, d), lambda i, f: ((i // per_seq) * 6 + k, 0, 0))

    return pl.pallas_call(
        _mlp_kernel,
        grid=(t // tm, ff // tf),
        in_specs=[
            pl.BlockSpec((tm, d), lambda i, f: (i, 0)),
            mod_spec(4), mod_spec(3), mod_spec(5),
            pl.BlockSpec((d, tf), lambda i, f: (0, f)),
            pl.BlockSpec((tf, d), lambda i, f: (f, 0)),
            pl.BlockSpec((1, d), lambda i, f: (0, 0)),
            pl.BlockSpec((1, d), lambda i, f: (0, 0)),
        ],
        out_specs=pl.BlockSpec((tm, d), lambda i, f: (i, 0)),
        out_shape=jax.ShapeDtypeStruct((t, d), F32),
        scratch_shapes=[pltpu.VMEM((tm, d), BF16), pltpu.VMEM((tm, d), F32)],
        compiler_params=_cparams(("parallel", "arbitrary")),
        name="mlp",
    )(x2, mods, mods, mods, w1, w2, ln_g.reshape(1, d), ln_b.reshape(1, d))


def _split3(x):
    hi = x.astype(BF16)
    r1 = x - hi.astype(F32)
    mid = r1.astype(BF16)
    lo = (r1 - mid.astype(F32)).astype(BF16)
    return hi, mid, lo


def _shift_rows(x, prev_tail, k):
    rolled = pltpu.roll(x, k, axis=0)
    top = jnp.where(lax.broadcasted_iota(jnp.int32, prev_tail.shape, 0) < k,
                    pltpu.roll(prev_tail, k, axis=0), rolled[0:SUBLANE])
    return jnp.concatenate([top, rolled[SUBLANE:]], axis=0)


def _ssd_kernel(z_ref, x_ref, b_ref, c_ref, dt_ref, wx_ref, wb_ref, wc_ref, bx_ref, bb_ref, bc_ref,
                dtb_ref, alog_ref, dsk_ref, e_ref, nw_ref, o_ref, tail, state, dt_sp, acs_sp):
    q = CHUNK
    lb = x_ref.shape[0]
    gw = GROUP_W
    ns = SSM_STATE
    g = pl.program_id(2)
    n_chunks = lb // q

    @pl.when(pl.program_id(1) == 0)
    def _():
        tail[g] = jnp.zeros(tail.shape[1:], F32)
        state[g] = jnp.zeros(state.shape[1:], F32)

    @pl.when(g == 0)
    def _():
        dt_raw = dt_ref[...] + dtb_ref[...]
        dt = jnp.maximum(dt_raw, 0.0) + jnp.log(1.0 + jnp.exp(-jnp.abs(dt_raw)))
        a2 = dt * (-jnp.exp(alog_ref[...]) * math.log2(math.e))
        tri = (lax.broadcasted_iota(jnp.int32, (q, q), 0)
               >= lax.broadcasted_iota(jnp.int32, (q, q), 1)).astype(BF16)
        for i, piece in enumerate(_split3(dt)):
            dt_sp[i] = piece
        for c in range(n_chunks):
            acs = sum(_dot(tri, piece) for piece in _split3(a2[c * q:(c + 1) * q]))
            for i, piece in enumerate(_split3(acs)):
                acs_sp[i, c * q:(c + 1) * q, :] = piece

    e = e_ref[0]
    dt_e = sum(_dot(dt_sp[i], e) for i in range(3))
    acs_e = sum(_dot(acs_sp[i], e) for i in range(3))
    d_e = sum(_dot(jnp.broadcast_to(piece, (2 * SUBLANE, LANE)), e) for piece in _split3(dsk_ref[...]))[0:1]
    nw = nw_ref[...]

    xin = jnp.concatenate([x_ref[...], b_ref[...], c_ref[...]], axis=1).astype(F32)
    w = jnp.concatenate([wx_ref[...], wb_ref[...], wc_ref[...]], axis=1)
    conv = jnp.concatenate([bx_ref[...], bb_ref[...], bc_ref[...]], axis=1)
    prev_tail = tail[g]
    conv = conv + w[SSM_CONV - 1:SSM_CONV, :] * xin
    for k in range(1, SSM_CONV):
        conv = conv + w[SSM_CONV - 1 - k:SSM_CONV - k, :] * _shift_rows(xin, prev_tail, k)
    tail[g] = xin[lb - SUBLANE:lb, :]
    xc = conv * _sigmoid(conv)
    xs = xc[:, 0:gw]
    bm = xc[:, gw:gw + ns]
    cm = xc[:, gw + ns:gw + 2 * ns]

    li = lax.broadcasted_iota(jnp.int32, (q, gw), 0)
    si = lax.broadcasted_iota(jnp.int32, (q, gw), 1) & (q - 1)
    diag = li == si
    causal = li >= si
    br = lax.broadcasted_iota(jnp.int32, (gw, gw), 0) // SSM_HEAD_DIM
    bc = lax.broadcasted_iota(jnp.int32, (gw, gw), 1) // SSM_HEAD_DIM
    blockdiag = br == bc

    st = state[g]
    for c in range(n_chunks):
        r0 = c * q
        xs_c = xs[r0:r0 + q]
        bm_c = bm[r0:r0 + q].astype(BF16)
        cm_c = cm[r0:r0 + q].astype(BF16)
        acs = acs_e[r0:r0 + q]
        last = acs[q - 1:q, :]
        arow = jnp.sum(jnp.where(diag, acs, 0.0), axis=0, keepdims=True)
        decay = jnp.exp2(jnp.where(causal, acs - arow, -jnp.inf))
        b4 = jnp.concatenate([bm_c] * SSM_HPG, axis=0)
        cb = lax.dot_general(cm_c, b4, (((1,), (1,)), ((), ())), preferred_element_type=F32)
        m = (cb * decay).astype(BF16)
        xd = xs_c * dt_e[r0:r0 + q]
        xd_b = xd.astype(BF16)
        xbd = jnp.where(blockdiag, jnp.concatenate([xd_b] * SSM_HPG, axis=0), jnp.zeros((), BF16))
        y = _dot(m, xbd)
        y = y + _dot(cm_c, st.astype(BF16)) * jnp.exp2(acs)
        xds = (xd * jnp.exp2(last - acs)).astype(BF16)
        local = lax.dot_general(bm_c, xds, (((0,), (0,)), ((), ())), preferred_element_type=F32)
        st = jnp.exp2(last) * st + local
        y = y + d_e * xs_c
        z_c = z_ref[r0:r0 + q, :].astype(F32)
        gz = y * (z_c * _sigmoid(z_c))
        ms = jnp.mean(gz * gz, axis=-1, keepdims=True)
        o_ref[r0:r0 + q, :] = (gz * lax.rsqrt(ms + EPS) * nw).astype(o_ref.dtype)
    state[g] = st


def _ssd(zx, dt, conv_w, conv_b, dt_bias, a_log, d_skip, norm_w, seq):
    t = zx.shape[0]
    gw, ns, g = GROUP_W, SSM_STATE, SSM_GROUPS
    d_inner = g * gw
    lb = min(512, seq)
    nj = seq // lb
    bsz = t // seq
    heads = dt_bias.shape[0]

    def pad_lanes(v):
        return jnp.zeros((1, LANE), F32).at[0, :heads].set(v)

    wpad = jnp.zeros((SUBLANE, conv_w.shape[1]), F32).at[:SSM_CONV].set(conv_w)
    cb2 = conv_b.reshape(1, -1)
    hh = jnp.arange(LANE)[None, :, None]
    cc = jnp.arange(gw)[None, None, :] // SSM_HEAD_DIM
    gg = jnp.arange(g)[:, None, None]
    expand = (hh == gg * SSM_HPG + cc).astype(BF16)

    row = lambda b, j, gi: b * nj + j
    xoff, boff, coff = d_inner // gw, (2 * d_inner) // ns, (2 * d_inner + g * ns) // ns
    wb_off, wc_off = d_inner // ns, (d_inner + g * ns) // ns
    in_specs = [
        pl.BlockSpec((lb, gw), lambda b, j, gi: (row(b, j, gi), gi)),
        pl.BlockSpec((lb, gw), lambda b, j, gi: (row(b, j, gi), xoff + gi)),
        pl.BlockSpec((lb, ns), lambda b, j, gi: (row(b, j, gi), boff + gi)),
        pl.BlockSpec((lb, ns), lambda b, j, gi: (row(b, j, gi), coff + gi)),
        pl.BlockSpec((lb, LANE), lambda b, j, gi: (row(b, j, gi), 0)),
        pl.BlockSpec((SUBLANE, gw), lambda b, j, gi: (0, gi)),
        pl.BlockSpec((SUBLANE, ns), lambda b, j, gi: (0, wb_off + gi)),
        pl.BlockSpec((SUBLANE, ns), lambda b, j, gi: (0, wc_off + gi)),
        pl.BlockSpec((1, gw), lambda b, j, gi: (0, gi)),
        pl.BlockSpec((1, ns), lambda b, j, gi: (0, wb_off + gi)),
        pl.BlockSpec((1, ns), lambda b, j, gi: (0, wc_off + gi)),
        pl.BlockSpec((1, LANE), lambda b, j, gi: (0, 0)),
        pl.BlockSpec((1, LANE), lambda b, j, gi: (0, 0)),
        pl.BlockSpec((1, LANE), lambda b, j, gi: (0, 0)),
        pl.BlockSpec((1, LANE, gw), lambda b, j, gi: (gi, 0, 0)),
        pl.BlockSpec((1, gw), lambda b, j, gi: (0, gi)),
    ]
    return pl.pallas_call(
        _ssd_kernel,
        grid=(bsz, nj, g),
        in_specs=in_specs,
        out_specs=pl.BlockSpec((lb, gw), lambda b, j, gi: (row(b, j, gi), gi)),
        out_shape=jax.ShapeDtypeStruct((t, d_inner), BF16),
        scratch_shapes=[
            pltpu.VMEM((g, SUBLANE, gw + 2 * ns), F32),
            pltpu.VMEM((g, ns, gw), F32),
            pltpu.VMEM((3, lb, LANE), BF16),
            pltpu.VMEM((3, lb, LANE), BF16),
        ],
        compiler_params=_cparams(("parallel", "arbitrary", "arbitrary")),
        name="ssd",
    )(zx, zx, zx, zx, dt, wpad, wpad, wpad, cb2, cb2, cb2,
      pad_lanes(dt_bias), pad_lanes(a_log), pad_lanes(d_skip), expand, norm_w.reshape(1, -1))


def _attn_kernel(q_ref, k_ref, v_ref, lamp_ref, subw_ref, o_ref, vt_ref, acc_ref, m_ref,
                 sa_ref, bma_ref, sb_ref, bmb_ref,
                 *, lambda_init):
    tq = q_ref.shape[1]
    tk = vt_ref.shape[2]
    seq = k_ref.shape[1]
    dh = DIFF_HEAD_DIM
    dv = DIFF_V_DIM
    qi = pl.program_id(2)

    @pl.when(qi == 0)
    def _():
        for i in range(seq // tk):
            vt_ref[i, 0:dv, :] = v_ref[0, i * tk:(i + 1) * tk, :].astype(F32).T.astype(BF16)
            vt_ref[i, dv:, :] = jnp.ones((vt_ref.shape[1] - dv, tk), BF16)

    qb = q_ref[0]
    lane = lax.broadcasted_iota(jnp.int32, qb.shape, 1)
    zero = jnp.zeros((), BF16)
    qs = (jnp.where(lane < dh, qb, zero), jnp.where(lane >= dh, qb, zero))

    m_ref[...] = jnp.full(m_ref.shape, -jnp.inf, F32)
    acc_ref[...] = jnp.zeros(acc_ref.shape, F32)

    def scores(kj, s_ref, bm_ref, masked=False, cols=(0, tq)):
        c0, c1 = cols
        r0 = pl.multiple_of(kj * tk, tk)
        kb = k_ref[0, pl.ds(r0, tk), :]
        for i in range(2):
            s = lax.dot_general(kb, qs[i][c0:c1], (((1,), (1,)), ((), ())), preferred_element_type=F32)
            if masked:
                kc = lax.broadcasted_iota(jnp.int32, s.shape, 0) // CHUNK
                qc = lax.broadcasted_iota(jnp.int32, s.shape, 1) // CHUNK
                s = jnp.where(kc <= qc, s, -jnp.inf)
            s_ref[i, :, c0:c1] = s
            bm_ref[i, :, c0:c1] = jnp.max(s, axis=0, keepdims=True)

    def update(kj, s_ref, bm_ref, cols=(0, tq)):
        c0, c1 = cols
        vtb = vt_ref[kj]
        for i in range(2):
            m_old = m_ref[i, :, c0:c1]
            m_new = jnp.maximum(m_old, bm_ref[i, :, c0:c1])
            alpha = jnp.exp2(m_old - m_new)
            p = jnp.exp2(s_ref[i, :, c0:c1] - m_new)
            acc_ref[i, :, c0:c1] = alpha * acc_ref[i, :, c0:c1] + _dot(vtb, p.astype(BF16))
            m_ref[i, :, c0:c1] = m_new

    a_buf, b_buf = (sa_ref, bma_ref), (sb_ref, bmb_ref)
    upper = (tk, tq)
    d0 = 2 * qi

    @pl.when(qi == 0)
    def _():
        scores(0, *a_buf, masked=True)
        scores(1, *b_buf, masked=True, cols=upper)
        update(0, *a_buf)
        update(1, *b_buf, cols=upper)

    @pl.when(qi > 0)
    def _():
        scores(0, *a_buf)

        def body(u, carry):
            scores(2 * u + 1, *b_buf)
            update(2 * u, *a_buf)
            scores(2 * u + 2, *a_buf)
            update(2 * u + 1, *b_buf)
            return carry

        lax.fori_loop(0, qi - 1, body, 0)
        scores(d0 - 1, *b_buf)
        update(d0 - 2, *a_buf)
        scores(d0, *a_buf, masked=True)
        update(d0 - 1, *b_buf)
        scores(d0 + 1, *b_buf, masked=True, cols=upper)
        update(d0, *a_buf)
        update(d0 + 1, *b_buf, cols=upper)

    lp = lamp_ref[...]
    lam = (jnp.exp(jnp.sum(lp[0:1] * lp[1:2], axis=-1, keepdims=True))
           - jnp.exp(jnp.sum(lp[2:3] * lp[3:4], axis=-1, keepdims=True)) + lambda_init)
    o = (acc_ref[0, 0:dv, :] / acc_ref[0, dv:dv + 1, :]
         - lam * (acc_ref[1, 0:dv, :] / acc_ref[1, dv:dv + 1, :]))
    ms = jnp.mean(o * o, axis=0, keepdims=True)
    sw = jnp.concatenate([subw_ref[...]] * (tq // LANE), axis=1)
    o = o * lax.rsqrt(ms + EPS) * sw * (1.0 - lambda_init)
    o_ref[0] = o.T.astype(o_ref.dtype)


def _attn(qkv, lq1, lk1, lq2, lk2, subln_w, lambda_init, bsz, seq):
    d3 = qkv.shape[1]
    d = d3 // 3
    heads = d // (2 * DIFF_HEAD_DIM)
    tk = min(512, seq // 2)
    tq = 2 * tk
    assert tk % CHUNK == 0 and seq % tq == 0
    qkv3 = qkv.reshape(bsz, seq, d3)
    lamp = jnp.zeros((SUBLANE, LANE), F32)
    for r, v in enumerate((lq1, lk1, lq2, lk2)):
        lamp = lamp.at[r, :v.shape[0]].set(v)
    subw = jnp.broadcast_to(subln_w[:, None], (DIFF_V_DIM, LANE)).astype(F32)
    out = pl.pallas_call(
        functools.partial(_attn_kernel, lambda_init=lambda_init),
        grid=(bsz, heads, seq // tq),
        in_specs=[
            pl.BlockSpec((1, tq, LANE), lambda b, h, i: (b, i, h)),
            pl.BlockSpec((1, seq, LANE), lambda b, h, i: (b, 0, heads + h)),
            pl.BlockSpec((1, seq, LANE), lambda b, h, i: (b, 0, 2 * heads + h)),
            pl.BlockSpec((SUBLANE, LANE), lambda b, h, i: (0, 0)),
            pl.BlockSpec((DIFF_V_DIM, LANE), lambda b, h, i: (0, 0)),
        ],
        out_specs=pl.BlockSpec((1, tq, LANE), lambda b, h, i: (b, i, h)),
        out_shape=jax.ShapeDtypeStruct((bsz, seq, d), BF16),
        scratch_shapes=[
            pltpu.VMEM((seq // tk, DIFF_V_DIM + ATTN_ONES_ROWS, tk), BF16),
            pltpu.VMEM((2, DIFF_V_DIM + ATTN_ONES_ROWS, tq), F32),
            pltpu.VMEM((2, 1, tq), F32),
            pltpu.VMEM((2, tk, tq), F32), pltpu.VMEM((2, 1, tq), F32),
            pltpu.VMEM((2, tk, tq), F32), pltpu.VMEM((2, 1, tq), F32),
        ],
        compiler_params=_cparams(("parallel", "parallel", "arbitrary")),
        name="attn",
    )(qkv3, qkv3, qkv3, lamp, subw)
    return out.reshape(bsz * seq, d)


def kernel(x, c, ada_w, ada_b, ln1_g, ln1_b, ln2_g, ln2_b, mlp_w1, mlp_w2, ssm_w_in, ssm_conv_w, ssm_conv_b, ssm_dt_bias, ssm_a_log, ssm_d, ssm_norm_w, ssm_w_out, attn_w_qkv, attn_lq1, attn_lk1, attn_lq2, attn_lk2, attn_subln_w, attn_w_out):
    bsz, seq, d = x.shape
    depth = ada_w.shape[0]
    mods = _mods(c, ada_w, ada_b)
    x2 = x.reshape(bsz * seq, d)
    n_main = SSM_GROUPS * GROUP_W * 2 + 2 * SSM_GROUPS * SSM_STATE
    for i in range(depth):
        j = i // 2
        if i % 2 == 0:
            w_in = ssm_w_in[j]
            heads = w_in.shape[1] - n_main
            w_dt = jnp.zeros((d, LANE), F32).at[:, :heads].set(w_in[:, n_main:])
            zx, dt = _proj(x2, mods[i], 1, 0, w_in[:, :n_main].astype(BF16), BF16, seq, w_dt=w_dt.astype(BF16))
            a = _ssd(zx, dt, ssm_conv_w[j], ssm_conv_b[j], ssm_dt_bias[j], ssm_a_log[j], ssm_d[j],
                     ssm_norm_w[j], seq)
            w_out = ssm_w_out[j]
        else:
            lambda_init = 0.8 - 0.6 * math.exp(-0.3 * i)
            col_scale = jnp.concatenate([jnp.full((d,), math.log2(math.e) * DIFF_HEAD_DIM ** -0.5, F32),
                                         jnp.ones((2 * d,), F32)])
            qkv = _proj(x2, mods[i], 1, 0, attn_w_qkv[j].astype(BF16), BF16, seq, col_scale=col_scale)
            a = _attn(qkv, attn_lq1[j], attn_lk1[j], attn_lq2[j], attn_lk2[j], attn_subln_w[j],
                      lambda_init, bsz, seq)
            w_out = attn_w_out[j]
        x2 = _outproj(a, w_out.astype(BF16), x2, mods[i], 2, ln1_g[i], ln1_b[i], seq)
        x2 = _mlp(x2, mods[i], mlp_w1[i].astype(BF16), mlp_w2[i].astype(BF16), ln2_g[i], ln2_b[i], seq)
    return x2.reshape(bsz, seq, d)
```

```python
import functools
import math

import jax
import jax.numpy as jnp
from jax import lax
from jax.experimental import pallas as pl
from jax.experimental.pallas import tpu as pltpu

F32 = jnp.float32
BF16 = jnp.bfloat16

DEPTH = 2
CHUNK = 64
SSM_HEAD_DIM = 64
SSM_GROUPS = 8
SSM_HPG = 4
SSM_STATE = 128
SSM_CONV = 4
GROUP_W = SSM_HPG * SSM_HEAD_DIM
DIFF_HEAD_DIM = 64
DIFF_V_DIM = 128
ATTN_ONES_ROWS = 16
DEEPNORM_ALPHA = (2 * DEPTH) ** 0.25
EPS = 1e-5

LANE = 128
SUBLANE = 8
VMEM_LIMIT = 56 * 1024 * 1024
MLP_TM, MLP_NSUB, MLP_TF = 1024, 4, 1024
OUTPROJ_TM, OUTPROJ_NSUB = 1024, 4
PROJ_TM, PROJ_NSUB, PROJ_TN = 1024, 2, 1024


def _cparams(sem):
    return pltpu.CompilerParams(dimension_semantics=sem, vmem_limit_bytes=VMEM_LIMIT)


def _sigmoid(x):
    return 1.0 / (1.0 + jnp.exp(-x))


def _layer_norm(v, g, b):
    mu = jnp.mean(v, axis=-1, keepdims=True)
    vc = v - mu
    var = jnp.mean(vc * vc, axis=-1, keepdims=True)
    return vc * lax.rsqrt(var + EPS) * g + b


def _dot(a, b):
    return jnp.dot(a, b, preferred_element_type=F32)


def _mods_kernel(c_ref, w_ref, b_ref, o_ref):
    c = c_ref[...]
    cond = c * _sigmoid(c)
    o_ref[0] = _dot(cond.astype(BF16), w_ref[0].astype(BF16)) + b_ref[0]


def _mods(c, ada_w, ada_b):
    depth, d, n = ada_w.shape
    bsz = c.shape[0]
    cp = jnp.zeros((SUBLANE, d), F32).at[:bsz].set(c)
    tn = 1024
    out = pl.pallas_call(
        _mods_kernel,
        grid=(depth, n // tn),
        in_specs=[
            pl.BlockSpec((SUBLANE, d), lambda l, j: (0, 0)),
            pl.BlockSpec((1, d, tn), lambda l, j: (l, 0, j)),
            pl.BlockSpec((1, 1, tn), lambda l, j: (l, 0, j)),
        ],
        out_specs=pl.BlockSpec((1, SUBLANE, tn), lambda l, j: (l, 0, j)),
        out_shape=jax.ShapeDtypeStruct((depth, SUBLANE, n), F32),
        compiler_params=_cparams(("parallel", "parallel")),
        name="mods",
    )(cp, ada_w, ada_b.reshape(depth, 1, n))
    return out[:, :bsz].reshape(depth, bsz * 6, 1, d)


def _proj_body(x_ref, sc_ref, sh_ref, w_ref, o_ref, n_sub, tn, cs_ref=None, wdt_ref=None, dt_ref=None):
    rows = x_ref.shape[0] // n_sub
    n = w_ref.shape[1]
    for s in range(n_sub):
        r0 = s * rows
        hb = (x_ref[r0:r0 + rows, :] * (1.0 + sc_ref[0]) + sh_ref[0]).astype(BF16)
        for j in range(n // tn):
            y = _dot(hb, w_ref[:, j * tn:(j + 1) * tn])
            if cs_ref is not None:
                y = y * cs_ref[:, j * tn:(j + 1) * tn]
            o_ref[r0:r0 + rows, j * tn:(j + 1) * tn] = y.astype(o_ref.dtype)
        if dt_ref is not None:
            dt_ref[r0:r0 + rows, :] = _dot(hb, wdt_ref[...])


def _proj_scaled_kernel(x_ref, sc_ref, sh_ref, w_ref, cs_ref, o_ref, *, n_sub, tn):
    _proj_body(x_ref, sc_ref, sh_ref, w_ref, o_ref, n_sub, tn, cs_ref=cs_ref)


def _proj_dt_kernel(x_ref, sc_ref, sh_ref, w_ref, wdt_ref, o_ref, dt_ref, *, n_sub, tn):
    _proj_body(x_ref, sc_ref, sh_ref, w_ref, o_ref, n_sub, tn, wdt_ref=wdt_ref, dt_ref=dt_ref)


def _proj(x2, mods, k_scale, k_shift, w, out_dtype, seq, w_dt=None, col_scale=None):
    t, d = x2.shape
    n = w.shape[1]
    tm = min(PROJ_TM, seq)
    per_seq = seq // tm
    resident = dict(pipeline_mode=pl.Buffered(1))
    in_specs = [
        pl.BlockSpec((tm, d), lambda i: (i, 0)),
        pl.BlockSpec((1, 1, d), lambda i: ((i // per_seq) * 6 + k_scale, 0, 0)),
        pl.BlockSpec((1, 1, d), lambda i: ((i // per_seq) * 6 + k_shift, 0, 0)),
        pl.BlockSpec((d, n), lambda i: (0, 0), **resident),
    ]
    out_spec = pl.BlockSpec((tm, n), lambda i: (i, 0))
    out_shape = jax.ShapeDtypeStruct((t, n), out_dtype)
    static = dict(n_sub=PROJ_NSUB, tn=min(PROJ_TN, n))
    cp = _cparams(("parallel",))
    if w_dt is None:
        return pl.pallas_call(
            functools.partial(_proj_scaled_kernel, **static),
            grid=(t // tm,),
            in_specs=in_specs + [pl.BlockSpec((1, n), lambda i: (0, 0))],
            out_specs=out_spec,
            out_shape=out_shape,
            compiler_params=cp,
            name="proj",
        )(x2, mods, mods, w, col_scale.reshape(1, n))
    ndt = w_dt.shape[1]
    return pl.pallas_call(
        functools.partial(_proj_dt_kernel, **static),
        grid=(t // tm,),
        in_specs=in_specs + [pl.BlockSpec((d, ndt), lambda i: (0, 0), **resident)],
        out_specs=[out_spec, pl.BlockSpec((tm, ndt), lambda i: (i, 0))],
        out_shape=[out_shape, jax.ShapeDtypeStruct((t, ndt), F32)],
        compiler_params=cp,
        name="proj_dt",
    )(x2, mods, mods, w, w_dt)


def _outproj_kernel(a_ref, w_ref, x_ref, g_ref, lng_ref, lnb_ref, o_ref, *, n_sub):
    rows = x_ref.shape[0] // n_sub
    for s in range(n_sub):
        r0 = s * rows
        y = _dot(a_ref[r0:r0 + rows, :], w_ref[...])
        v = DEEPNORM_ALPHA * x_ref[r0:r0 + rows, :] + g_ref[0] * y
        o_ref[r0:r0 + rows, :] = _layer_norm(v, lng_ref[...], lnb_ref[...])


def _outproj(a, w, x2, mods, k_gate, ln_g, ln_b, seq):
    t, d = x2.shape
    k = a.shape[1]
    tm = min(OUTPROJ_TM, seq)
    per_seq = seq // tm
    return pl.pallas_call(
        functools.partial(_outproj_kernel, n_sub=OUTPROJ_NSUB),
        grid=(t // tm,),
        in_specs=[
            pl.BlockSpec((tm, k), lambda i: (i, 0)),
            pl.BlockSpec((k, d), lambda i: (0, 0), pipeline_mode=pl.Buffered(1)),
            pl.BlockSpec((tm, d), lambda i: (i, 0)),
            pl.BlockSpec((1, 1, d), lambda i: ((i // per_seq) * 6 + k_gate, 0, 0)),
            pl.BlockSpec((1, d), lambda i: (0, 0)),
            pl.BlockSpec((1, d), lambda i: (0, 0)),
        ],
        out_specs=pl.BlockSpec((tm, d), lambda i: (i, 0)),
        out_shape=jax.ShapeDtypeStruct((t, d), F32),
        compiler_params=_cparams(("parallel",)),
        name="outproj",
    )(a, w, x2, mods, ln_g.reshape(1, d), ln_b.reshape(1, d))


def _mlp_kernel(x_ref, sc_ref, sh_ref, g_ref, w1_ref, w2_ref, lng_ref, lnb_ref, o_ref, *, n_sub, tf):
    tm = x_ref.shape[0]
    ff = w1_ref.shape[1]
    rows = tm // n_sub
    for s in range(n_sub):
        r0 = s * rows
        x = x_ref[r0:r0 + rows, :]
        hb = (x * (1.0 + sc_ref[0]) + sh_ref[0]).astype(BF16)
        acc = None
        for f in range(ff // tf):
            u = jnp.maximum(_dot(hb, w1_ref[:, f * tf:(f + 1) * tf]), 0.0)
            part = _dot((u * u).astype(BF16), w2_ref[f * tf:(f + 1) * tf, :])
            acc = part if acc is None else acc + part
        v = DEEPNORM_ALPHA * x + g_ref[0] * acc
        o_ref[r0:r0 + rows, :] = _layer_norm(v, lng_ref[...], lnb_ref[...])


def _mlp(x2, mods, w1, w2, ln_g, ln_b, seq):
    t, d = x2.shape
    ff = w1.shape[1]
    tm = min(MLP_TM, seq)
    per_seq = seq // tm

    def mod_spec(k):
        return pl.BlockSpec((1, 1, d), lambda i: ((i // per_seq) * 6 + k, 0, 0))

    resident = dict(pipeline_mode=pl.Buffered(1))
    return pl.pallas_call(
        functools.partial(_mlp_kernel, n_sub=MLP_NSUB, tf=min(MLP_TF, ff)),
        grid=(t // tm,),
        in_specs=[
            pl.BlockSpec((tm, d), lambda i: (i, 0)),
            mod_spec(4), mod_spec(3), mod_spec(5),
            pl.BlockSpec((d, ff), lambda i: (0, 0), **resident),
            pl.BlockSpec((ff, d), lambda i: (0, 0), **resident),
            pl.BlockSpec((1, d), lambda i: (0, 0)),
            pl.BlockSpec((1, d), lambda i: (0, 0)),
        ],
        out_specs=pl.BlockSpec((tm, d), lambda i: (i, 0)),
        out_shape=jax.ShapeDtypeStruct((t, d), F32),
        compiler_params=_cparams(("parallel",)),
        name="mlp",
    )(x2, mods, mods, mods, w1, w2, ln_g.reshape(1, d), ln_b.reshape(1, d))


def _split3(x):
    hi = x.astype(BF16)
    r1 = x - hi.astype(F32)
    mid = r1.astype(BF16)
    lo = (r1 - mid.astype(F32)).astype(BF16)
    return hi, mid, lo


def _shift_rows(x, prev_tail, k):
    rolled = pltpu.roll(x, k, axis=0)
    top = jnp.where(lax.broadcasted_iota(jnp.int32, prev_tail.shape, 0) < k,
                    pltpu.roll(prev_tail, k, axis=0), rolled[0:SUBLANE])
    return jnp.concatenate([top, rolled[SUBLANE:]], axis=0)


def _ssd_kernel(z_ref, x_ref, b_ref, c_ref, dt_ref, wx_ref, wb_ref, wc_ref, bx_ref, bb_ref, bc_ref,
                dtb_ref, alog_ref, dsk_ref, e_ref, nw_ref, o_ref, tail, state, dt_sp, acs_sp):
    q = CHUNK
    lb = x_ref.shape[0]
    gw = GROUP_W
    ns = SSM_STATE
    g = pl.program_id(2)
    n_chunks = lb // q

    @pl.when(pl.program_id(1) == 0)
    def _():
        tail[g] = jnp.zeros(tail.shape[1:], F32)
        state[g] = jnp.zeros(state.shape[1:], F32)

    @pl.when(g == 0)
    def _():
        dt_raw = dt_ref[...] + dtb_ref[...]
        dt = jnp.maximum(dt_raw, 0.0) + jnp.log(1.0 + jnp.exp(-jnp.abs(dt_raw)))
        a2 = dt * (-jnp.exp(alog_ref[...]) * math.log2(math.e))
        tri = (lax.broadcasted_iota(jnp.int32, (q, q), 0)
               >= lax.broadcasted_iota(jnp.int32, (q, q), 1)).astype(BF16)
        for i, piece in enumerate(_split3(dt)):
            dt_sp[i] = piece
        for c in range(n_chunks):
            acs = sum(_dot(tri, piece) for piece in _split3(a2[c * q:(c + 1) * q]))
            for i, piece in enumerate(_split3(acs)):
                acs_sp[i, c * q:(c + 1) * q, :] = piece

    e = e_ref[0]
    dt_e = sum(_dot(dt_sp[i], e) for i in range(3))
    acs_e = sum(_dot(acs_sp[i], e) for i in range(3))
    d_e = sum(_dot(jnp.broadcast_to(piece, (2 * SUBLANE, LANE)), e) for piece in _split3(dsk_ref[...]))[0:1]
    nw = nw_ref[...]

    xin = jnp.concatenate([x_ref[...], b_ref[...], c_ref[...]], axis=1).astype(F32)
    w = jnp.concatenate([wx_ref[...], wb_ref[...], wc_ref[...]], axis=1)
    conv = jnp.concatenate([bx_ref[...], bb_ref[...], bc_ref[...]], axis=1)
    prev_tail = tail[g]
    conv = conv + w[SSM_CONV - 1:SSM_CONV, :] * xin
    for k in range(1, SSM_CONV):
        conv = conv + w[SSM_CONV - 1 - k:SSM_CONV - k, :] * _shift_rows(xin, prev_tail, k)
    tail[g] = xin[lb - SUBLANE:lb, :]
    xc = conv * _sigmoid(conv)
    xs = xc[:, 0:gw]
    bm = xc[:, gw:gw + ns]
    cm = xc[:, gw + ns:gw + 2 * ns]

    li = lax.broadcasted_iota(jnp.int32, (q, gw), 0)
    si = lax.broadcasted_iota(jnp.int32, (q, gw), 1) & (q - 1)
    diag = li == si
    causal = li >= si
    br = lax.broadcasted_iota(jnp.int32, (gw, gw), 0) // SSM_HEAD_DIM
    bc = lax.broadcasted_iota(jnp.int32, (gw, gw), 1) // SSM_HEAD_DIM
    blockdiag = br == bc

    st = state[g]
    for c in range(n_chunks):
        r0 = c * q
        xs_c = xs[r0:r0 + q]
        bm_c = bm[r0:r0 + q].astype(BF16)
        cm_c = cm[r0:r0 + q].astype(BF16)
        acs = acs_e[r0:r0 + q]
        last = acs[q - 1:q, :]
        arow = jnp.sum(jnp.where(diag, acs, 0.0), axis=0, keepdims=True)
        decay = jnp.exp2(jnp.where(causal, acs - arow, -jnp.inf))
        b4 = jnp.concatenate([bm_c] * SSM_HPG, axis=0)
        cb = lax.dot_general(cm_c, b4, (((1,), (1,)), ((), ())), preferred_element_type=F32)
        m = (cb * decay).astype(BF16)
        xd = xs_c * dt_e[r0:r0 + q]
        xd_b = xd.astype(BF16)
        xbd = jnp.where(blockdiag, jnp.concatenate([xd_b] * SSM_HPG, axis=0), jnp.zeros((), BF16))
        y = _dot(m, xbd)
        y = y + _dot(cm_c, st.astype(BF16)) * jnp.exp2(acs)
        xds = (xd * jnp.exp2(last - acs)).astype(BF16)
        local = lax.dot_general(bm_c, xds, (((0,), (0,)), ((), ())), preferred_element_type=F32)
        st = jnp.exp2(last) * st + local
        y = y + d_e * xs_c
        z_c = z_ref[r0:r0 + q, :].astype(F32)
        gz = y * (z_c * _sigmoid(z_c))
        ms = jnp.mean(gz * gz, axis=-1, keepdims=True)
        o_ref[r0:r0 + q, :] = (gz * lax.rsqrt(ms + EPS) * nw).astype(o_ref.dtype)
    state[g] = st


def _ssd(zx, dt, conv_w, conv_b, dt_bias, a_log, d_skip, norm_w, seq):
    t = zx.shape[0]
    gw, ns, g = GROUP_W, SSM_STATE, SSM_GROUPS
    d_inner = g * gw
    lb = min(1024, seq)
    nj = seq // lb
    bsz = t // seq
    heads = dt_bias.shape[0]

    def pad_lanes(v):
        return jnp.zeros((1, LANE), F32).at[0, :heads].set(v)

    wpad = jnp.zeros((SUBLANE, conv_w.shape[1]), F32).at[:SSM_CONV].set(conv_w)
    cb2 = conv_b.reshape(1, -1)
    hh = jnp.arange(LANE)[None, :, None]
    cc = jnp.arange(gw)[None, None, :] // SSM_HEAD_DIM
    gg = jnp.arange(g)[:, None, None]
    expand = (hh == gg * SSM_HPG + cc).astype(BF16)

    row = lambda b, j, gi: b * nj + j
    xoff, boff, coff = d_inner // gw, (2 * d_inner) // ns, (2 * d_inner + g * ns) // ns
    wb_off, wc_off = d_inner // ns, (d_inner + g * ns) // ns
    in_specs = [
        pl.BlockSpec((lb, gw), lambda b, j, gi: (row(b, j, gi), gi)),
        pl.BlockSpec((lb, gw), lambda b, j, gi: (row(b, j, gi), xoff + gi)),
        pl.BlockSpec((lb, ns), lambda b, j, gi: (row(b, j, gi), boff + gi)),
        pl.BlockSpec((lb, ns), lambda b, j, gi: (row(b, j, gi), coff + gi)),
        pl.BlockSpec((lb, LANE), lambda b, j, gi: (row(b, j, gi), 0)),
        pl.BlockSpec((SUBLANE, gw), lambda b, j, gi: (0, gi)),
        pl.BlockSpec((SUBLANE, ns), lambda b, j, gi: (0, wb_off + gi)),
        pl.BlockSpec((SUBLANE, ns), lambda b, j, gi: (0, wc_off + gi)),
        pl.BlockSpec((1, gw), lambda b, j, gi: (0, gi)),
        pl.BlockSpec((1, ns), lambda b, j, gi: (0, wb_off + gi)),
        pl.BlockSpec((1, ns), lambda b, j, gi: (0, wc_off + gi)),
        pl.BlockSpec((1, LANE), lambda b, j, gi: (0, 0)),
        pl.BlockSpec((1, LANE), lambda b, j, gi: (0, 0)),
        pl.BlockSpec((1, LANE), lambda b, j, gi: (0, 0)),
        pl.BlockSpec((1, LANE, gw), lambda b, j, gi: (gi, 0, 0)),
        pl.BlockSpec((1, gw), lambda b, j, gi: (0, gi)),
    ]
    return pl.pallas_call(
        _ssd_kernel,
        grid=(bsz, nj, g),
        in_specs=in_specs,
        out_specs=pl.BlockSpec((lb, gw), lambda b, j, gi: (row(b, j, gi), gi)),
        out_shape=jax.ShapeDtypeStruct((t, d_inner), BF16),
        scratch_shapes=[
            pltpu.VMEM((g, SUBLANE, gw + 2 * ns), F32),
            pltpu.VMEM((g, ns, gw), F32),
            pltpu.VMEM((3, lb, LANE), BF16),
            pltpu.VMEM((3, lb, LANE), BF16),
        ],
        compiler_params=_cparams(("parallel", "arbitrary", "arbitrary")),
        name="ssd",
    )(zx, zx, zx, zx, dt, wpad, wpad, wpad, cb2, cb2, cb2,
      pad_lanes(dt_bias), pad_lanes(a_log), pad_lanes(d_skip), expand, norm_w.reshape(1, -1))


def _attn_kernel(q_ref, k_ref, v_ref, lamp_ref, subw_ref, o_ref, vt_ref, acc_ref, m_ref,
                 sa_ref, bma_ref, sb_ref, bmb_ref,
                 *, lambda_init):
    tq = q_ref.shape[1]
    tk = vt_ref.shape[2]
    seq = k_ref.shape[1]
    dh = DIFF_HEAD_DIM
    dv = DIFF_V_DIM
    qi = pl.program_id(2)

    @pl.when(qi == 0)
    def _():
        for i in range(seq // tk):
            vt_ref[i, 0:dv, :] = v_ref[0, i * tk:(i + 1) * tk, :].astype(F32).T.astype(BF16)
            vt_ref[i, dv:, :] = jnp.ones((vt_ref.shape[1] - dv, tk), BF16)

    qb = q_ref[0]
    lane = lax.broadcasted_iota(jnp.int32, qb.shape, 1)
    zero = jnp.zeros((), BF16)
    qs = (jnp.where(lane < dh, qb, zero), jnp.where(lane >= dh, qb, zero))

    m_ref[...] = jnp.full(m_ref.shape, -jnp.inf, F32)
    acc_ref[...] = jnp.zeros(acc_ref.shape, F32)

    def scores(kj, s_ref, bm_ref, masked=False, cols=(0, tq)):
        c0, c1 = cols
        r0 = pl.multiple_of(kj * tk, tk)
        kb = k_ref[0, pl.ds(r0, tk), :]
        for i in range(2):
            s = lax.dot_general(kb, qs[i][c0:c1], (((1,), (1,)), ((), ())), preferred_element_type=F32)
            if masked:
                kc = lax.broadcasted_iota(jnp.int32, s.shape, 0) // CHUNK
                qc = lax.broadcasted_iota(jnp.int32, s.shape, 1) // CHUNK
                s = jnp.where(kc <= qc, s, -jnp.inf)
            s_ref[i, :, c0:c1] = s
            bm_ref[i, :, c0:c1] = jnp.max(s, axis=0, keepdims=True)

    def update(kj, s_ref, bm_ref, cols=(0, tq)):
        c0, c1 = cols
        vtb = vt_ref[kj]
        for i in range(2):
            m_old = m_ref[i, :, c0:c1]
            m_new = jnp.maximum(m_old, bm_ref[i, :, c0:c1])
            alpha = jnp.exp2(m_old - m_new)
            p = jnp.exp2(s_ref[i, :, c0:c1] - m_new)
            acc_ref[i, :, c0:c1] = alpha * acc_ref[i, :, c0:c1] + _dot(vtb, p.astype(BF16))
            m_ref[i, :, c0:c1] = m_new

    a_buf, b_buf = (sa_ref, bma_ref), (sb_ref, bmb_ref)
    upper = (tk, tq)
    d0 = 2 * qi

    @pl.when(qi == 0)
    def _():
        scores(0, *a_buf, masked=True)
        scores(1, *b_buf, masked=True, cols=upper)
        update(0, *a_buf)
        update(1, *b_buf, cols=upper)

    @pl.when(qi > 0)
    def _():
        scores(0, *a_buf)

        def body(u, carry):
            scores(2 * u + 1, *b_buf)
            update(2 * u, *a_buf)
            scores(2 * u + 2, *a_buf)
            update(2 * u + 1, *b_buf)
            return carry

        lax.fori_loop(0, qi - 1, body, 0)
        scores(d0 - 1, *b_buf)
        update(d0 - 2, *a_buf)
        scores(d0, *a_buf, masked=True)
        update(d0 - 1, *b_buf)
        scores(d0 + 1, *b_buf, masked=True, cols=upper)
        update(d0, *a_buf)
        update(d0 + 1, *b_buf, cols=upper)

    lp = lamp_ref[...]
    lam = (jnp.exp(jnp.sum(lp[0:1] * lp[1:2], axis=-1, keepdims=True))
           - jnp.exp(jnp.sum(lp[2:3] * lp[3:4], axis=-1, keepdims=True)) + lambda_init)
    o = (acc_ref[0, 0:dv, :] / acc_ref[0, dv:dv + 1, :]
         - lam * (acc_ref[1, 0:dv, :] / acc_ref[1, dv:dv + 1, :]))
    ms = jnp.mean(o * o, axis=0, keepdims=True)
    sw = jnp.concatenate([subw_ref[...]] * (tq // LANE), axis=1)
    o = o * lax.rsqrt(ms + EPS) * sw * (1.0 - lambda_init)
    o_ref[0] = o.T.astype(o_ref.dtype)


def _attn(qkv, lq1, lk1, lq2, lk2, subln_w, lambda_init, bsz, seq):
    d3 = qkv.shape[1]
    d = d3 // 3
    heads = d // (2 * DIFF_HEAD_DIM)
    tk = min(512, seq // 2)
    tq = 2 * tk
    assert tk % CHUNK == 0 and seq % tq == 0
    qkv3 = qkv.reshape(bsz, seq, d3)
    lamp = jnp.zeros((SUBLANE, LANE), F32)
    for r, v in enumerate((lq1, lk1, lq2, lk2)):
        lamp = lamp.at[r, :v.shape[0]].set(v)
    subw = jnp.broadcast_to(subln_w[:, None], (DIFF_V_DIM, LANE)).astype(F32)
    out = pl.pallas_call(
        functools.partial(_attn_kernel, lambda_init=lambda_init),
        grid=(bsz, heads, seq // tq),
        in_specs=[
            pl.BlockSpec((1, tq, LANE), lambda b, h, i: (b, i, h)),
            pl.BlockSpec((1, seq, LANE), lambda b, h, i: (b, 0, heads + h)),
            pl.BlockSpec((1, seq, LANE), lambda b, h, i: (b, 0, 2 * heads + h)),
            pl.BlockSpec((SUBLANE, LANE), lambda b, h, i: (0, 0)),
            pl.BlockSpec((DIFF_V_DIM, LANE), lambda b, h, i: (0, 0)),
        ],
        out_specs=pl.BlockSpec((1, tq, LANE), lambda b, h, i: (b, i, h)),
        out_shape=jax.ShapeDtypeStruct((bsz, seq, d), BF16),
        scratch_shapes=[
            pltpu.VMEM((seq // tk, DIFF_V_DIM + ATTN_ONES_ROWS, tk), BF16),
            pltpu.VMEM((2, DIFF_V_DIM + ATTN_ONES_ROWS, tq), F32),
            pltpu.VMEM((2, 1, tq), F32),
            pltpu.VMEM((2, tk, tq), F32), pltpu.VMEM((2, 1, tq), F32),
            pltpu.VMEM((2, tk, tq), F32), pltpu.VMEM((2, 1, tq), F32),
        ],
        compiler_params=_cparams(("parallel", "parallel", "arbitrary")),
        name="attn",
    )(qkv3, qkv3, qkv3, lamp, subw)
    return out.reshape(bsz * seq, d)


def kernel(x, c, ada_w, ada_b, ln1_g, ln1_b, ln2_g, ln2_b, mlp_w1, mlp_w2, ssm_w_in, ssm_conv_w, ssm_conv_b, ssm_dt_bias, ssm_a_log, ssm_d, ssm_norm_w, ssm_w_out, attn_w_qkv, attn_lq1, attn_lk1, attn_lq2, attn_lk2, attn_subln_w, attn_w_out):
    bsz, seq, d = x.shape
    depth = ada_w.shape[0]
    mods = _mods(c, ada_w, ada_b)
    x2 = x.reshape(bsz * seq, d)
    n_main = SSM_GROUPS * GROUP_W * 2 + 2 * SSM_GROUPS * SSM_STATE
    for i in range(depth):
        j = i // 2
        if i % 2 == 0:
            w_in = ssm_w_in[j]
            heads = w_in.shape[1] - n_main
            w_dt = jnp.zeros((d, LANE), F32).at[:, :heads].set(w_in[:, n_main:])
            zx, dt = _proj(x2, mods[i], 1, 0, w_in[:, :n_main].astype(BF16), BF16, seq, w_dt=w_dt.astype(BF16))
            a = _ssd(zx, dt, ssm_conv_w[j], ssm_conv_b[j], ssm_dt_bias[j], ssm_a_log[j], ssm_d[j],
                     ssm_norm_w[j], seq)
            w_out = ssm_w_out[j]
        else:
            lambda_init = 0.8 - 0.6 * math.exp(-0.3 * i)
            col_scale = jnp.concatenate([jnp.full((d,), math.log2(math.e) * DIFF_HEAD_DIM ** -0.5, F32),
                                         jnp.ones((2 * d,), F32)])
            qkv = _proj(x2, mods[i], 1, 0, attn_w_qkv[j].astype(BF16), BF16, seq, col_scale=col_scale)
            a = _attn(qkv, attn_lq1[j], attn_lk1[j], attn_lq2[j], attn_lk2[j], attn_subln_w[j],
                      lambda_init, bsz, seq)
            w_out = attn_w_out[j]
        x2 = _outproj(a, w_out.astype(BF16), x2, mods[i], 2, ln1_g[i], ln1_b[i], seq)
        x2 = _mlp(x2, mods[i], mlp_w1[i].astype(BF16), mlp_w2[i].astype(BF16), ln2_g[i], ln2_b[i], seq)
    return x2.reshape(bsz, seq, d)
```

```python
import functools
import itertools
import math

import jax
import jax.numpy as jnp
from jax import lax
from jax.experimental import pallas as pl
from jax.experimental.pallas import tpu as pltpu

F32 = jnp.float32
BF16 = jnp.bfloat16

DEPTH = 2
CHUNK = 64
SSM_HEAD_DIM = 64
SSM_GROUPS = 8
SSM_HPG = 4
SSM_STATE = 128
SSM_CONV = 4
GROUP_W = SSM_HPG * SSM_HEAD_DIM
DIFF_HEAD_DIM = 64
DIFF_V_DIM = 128
ATTN_COL_TILE = 256
ATTN_ONES_ROWS = 16
DEEPNORM_ALPHA = (2 * DEPTH) ** 0.25
EPS = 1e-5

LANE = 128
SUBLANE = 8
VMEM_LIMIT = 56 * 1024 * 1024
MLP_TM, MLP_NSUB, MLP_TF = 1024, 4, 1024
OUTPROJ_TM, OUTPROJ_NSUB = 1024, 4
PROJ_TM, PROJ_NSUB, PROJ_TN = 1024, 2, 1024


def _cparams(sem):
    return pltpu.CompilerParams(dimension_semantics=sem, vmem_limit_bytes=VMEM_LIMIT)


def _sigmoid(x):
    return 1.0 / (1.0 + jnp.exp(-x))


def _layer_norm(v, g, b):
    mu = jnp.mean(v, axis=-1, keepdims=True)
    vc = v - mu
    var = jnp.mean(vc * vc, axis=-1, keepdims=True)
    return vc * lax.rsqrt(var + EPS) * g + b


def _dot(a, b):
    return jnp.dot(a, b, preferred_element_type=F32)


def _mods_kernel(c_ref, w_ref, b_ref, o_ref):
    c = c_ref[...]
    cond = c * _sigmoid(c)
    o_ref[0] = _dot(cond.astype(BF16), w_ref[0].astype(BF16)) + b_ref[0]


def _mods(c, ada_w, ada_b):
    depth, d, n = ada_w.shape
    bsz = c.shape[0]
    cp = jnp.zeros((SUBLANE, d), F32).at[:bsz].set(c)
    tn = 1024
    out = pl.pallas_call(
        _mods_kernel,
        grid=(depth, n // tn),
        in_specs=[
            pl.BlockSpec((SUBLANE, d), lambda l, j: (0, 0)),
            pl.BlockSpec((1, d, tn), lambda l, j: (l, 0, j)),
            pl.BlockSpec((1, 1, tn), lambda l, j: (l, 0, j)),
        ],
        out_specs=pl.BlockSpec((1, SUBLANE, tn), lambda l, j: (l, 0, j)),
        out_shape=jax.ShapeDtypeStruct((depth, SUBLANE, n), F32),
        compiler_params=_cparams(("parallel", "parallel")),
        name="mods",
    )(cp, ada_w, ada_b.reshape(depth, 1, n))
    return out[:, :bsz].reshape(depth, bsz * 6, 1, d)


def _proj_body(x_ref, sc_ref, sh_ref, w_ref, o_ref, n_sub, tn, cs_ref=None, wdt_ref=None, dt_ref=None):
    rows = x_ref.shape[0] // n_sub
    n = w_ref.shape[1]
    for s in range(n_sub):
        r0 = s * rows
        hb = (x_ref[r0:r0 + rows, :] * (1.0 + sc_ref[0]) + sh_ref[0]).astype(BF16)
        for j in range(n // tn):
            y = _dot(hb, w_ref[:, j * tn:(j + 1) * tn])
            if cs_ref is not None:
                y = y * cs_ref[:, j * tn:(j + 1) * tn]
            o_ref[r0:r0 + rows, j * tn:(j + 1) * tn] = y.astype(o_ref.dtype)
        if dt_ref is not None:
            dt_ref[r0:r0 + rows, :] = _dot(hb, wdt_ref[...])


def _proj_scaled_kernel(x_ref, sc_ref, sh_ref, w_ref, cs_ref, o_ref, *, n_sub, tn):
    _proj_body(x_ref, sc_ref, sh_ref, w_ref, o_ref, n_sub, tn, cs_ref=cs_ref)


def _proj_dt_kernel(x_ref, sc_ref, sh_ref, w_ref, wdt_ref, o_ref, dt_ref, *, n_sub, tn):
    _proj_body(x_ref, sc_ref, sh_ref, w_ref, o_ref, n_sub, tn, wdt_ref=wdt_ref, dt_ref=dt_ref)


def _proj(x2, mods, k_scale, k_shift, w, out_dtype, seq, w_dt=None, col_scale=None):
    t, d = x2.shape
    n = w.shape[1]
    tm = min(PROJ_TM, seq)
    per_seq = seq // tm
    resident = dict(pipeline_mode=pl.Buffered(1))
    in_specs = [
        pl.BlockSpec((tm, d), lambda i: (i, 0)),
        pl.BlockSpec((1, 1, d), lambda i: ((i // per_seq) * 6 + k_scale, 0, 0)),
        pl.BlockSpec((1, 1, d), lambda i: ((i // per_seq) * 6 + k_shift, 0, 0)),
        pl.BlockSpec((d, n), lambda i: (0, 0), **resident),
    ]
    out_spec = pl.BlockSpec((tm, n), lambda i: (i, 0))
    out_shape = jax.ShapeDtypeStruct((t, n), out_dtype)
    static = dict(n_sub=PROJ_NSUB, tn=min(PROJ_TN, n))
    cp = _cparams(("parallel",))
    if w_dt is None:
        return pl.pallas_call(
            functools.partial(_proj_scaled_kernel, **static),
            grid=(t // tm,),
            in_specs=in_specs + [pl.BlockSpec((1, n), lambda i: (0, 0))],
            out_specs=out_spec,
            out_shape=out_shape,
            compiler_params=cp,
            name="proj",
        )(x2, mods, mods, w, col_scale.reshape(1, n))
    ndt = w_dt.shape[1]
    return pl.pallas_call(
        functools.partial(_proj_dt_kernel, **static),
        grid=(t // tm,),
        in_specs=in_specs + [pl.BlockSpec((d, ndt), lambda i: (0, 0), **resident)],
        out_specs=[out_spec, pl.BlockSpec((tm, ndt), lambda i: (i, 0))],
        out_shape=[out_shape, jax.ShapeDtypeStruct((t, ndt), F32)],
        compiler_params=cp,
        name="proj_dt",
    )(x2, mods, mods, w, w_dt)


def _outproj_kernel(a_ref, w_ref, x_ref, g_ref, lng_ref, lnb_ref, o_ref, *, n_sub):
    rows = x_ref.shape[0] // n_sub
    for s in range(n_sub):
        r0 = s * rows
        y = _dot(a_ref[r0:r0 + rows, :], w_ref[...])
        v = DEEPNORM_ALPHA * x_ref[r0:r0 + rows, :] + g_ref[0] * y
        o_ref[r0:r0 + rows, :] = _layer_norm(v, lng_ref[...], lnb_ref[...])


def _outproj(a, w, x2, mods, k_gate, ln_g, ln_b, seq):
    t, d = x2.shape
    k = a.shape[1]
    tm = min(OUTPROJ_TM, seq)
    per_seq = seq // tm
    return pl.pallas_call(
        functools.partial(_outproj_kernel, n_sub=OUTPROJ_NSUB),
        grid=(t // tm,),
        in_specs=[
            pl.BlockSpec((tm, k), lambda i: (i, 0)),
            pl.BlockSpec((k, d), lambda i: (0, 0), pipeline_mode=pl.Buffered(1)),
            pl.BlockSpec((tm, d), lambda i: (i, 0)),
            pl.BlockSpec((1, 1, d), lambda i: ((i // per_seq) * 6 + k_gate, 0, 0)),
            pl.BlockSpec((1, d), lambda i: (0, 0)),
            pl.BlockSpec((1, d), lambda i: (0, 0)),
        ],
        out_specs=pl.BlockSpec((tm, d), lambda i: (i, 0)),
        out_shape=jax.ShapeDtypeStruct((t, d), F32),
        compiler_params=_cparams(("parallel",)),
        name="outproj",
    )(a, w, x2, mods, ln_g.reshape(1, d), ln_b.reshape(1, d))


def _mlp_kernel(x_ref, sc_ref, sh_ref, g_ref, w1_ref, w2_ref, lng_ref, lnb_ref, o_ref, *, n_sub, tf):
    tm = x_ref.shape[0]
    ff = w1_ref.shape[1]
    rows = tm // n_sub
    for s in range(n_sub):
        r0 = s * rows
        x = x_ref[r0:r0 + rows, :]
        hb = (x * (1.0 + sc_ref[0]) + sh_ref[0]).astype(BF16)
        acc = None
        for f in range(ff // tf):
            u = jnp.maximum(_dot(hb, w1_ref[:, f * tf:(f + 1) * tf]), 0.0)
            part = _dot((u * u).astype(BF16), w2_ref[f * tf:(f + 1) * tf, :])
            acc = part if acc is None else acc + part
        v = DEEPNORM_ALPHA * x + g_ref[0] * acc
        o_ref[r0:r0 + rows, :] = _layer_norm(v, lng_ref[...], lnb_ref[...])


def _mlp(x2, mods, w1, w2, ln_g, ln_b, seq):
    t, d = x2.shape
    ff = w1.shape[1]
    tm = min(MLP_TM, seq)
    per_seq = seq // tm

    def mod_spec(k):
        return pl.BlockSpec((1, 1, d), lambda i: ((i // per_seq) * 6 + k, 0, 0))

    resident = dict(pipeline_mode=pl.Buffered(1))
    return pl.pallas_call(
        functools.partial(_mlp_kernel, n_sub=MLP_NSUB, tf=min(MLP_TF, ff)),
        grid=(t // tm,),
        in_specs=[
            pl.BlockSpec((tm, d), lambda i: (i, 0)),
            mod_spec(4), mod_spec(3), mod_spec(5),
            pl.BlockSpec((d, ff), lambda i: (0, 0), **resident),
            pl.BlockSpec((ff, d), lambda i: (0, 0), **resident),
            pl.BlockSpec((1, d), lambda i: (0, 0)),
            pl.BlockSpec((1, d), lambda i: (0, 0)),
        ],
        out_specs=pl.BlockSpec((tm, d), lambda i: (i, 0)),
        out_shape=jax.ShapeDtypeStruct((t, d), F32),
        compiler_params=_cparams(("parallel",)),
        name="mlp",
    )(x2, mods, mods, mods, w1, w2, ln_g.reshape(1, d), ln_b.reshape(1, d))


def _split3(x):
    hi = x.astype(BF16)
    r1 = x - hi.astype(F32)
    mid = r1.astype(BF16)
    lo = (r1 - mid.astype(F32)).astype(BF16)
    return hi, mid, lo


def _shift_rows(x, prev_tail, k):
    rolled = pltpu.roll(x, k, axis=0)
    top = jnp.where(lax.broadcasted_iota(jnp.int32, prev_tail.shape, 0) < k,
                    pltpu.roll(prev_tail, k, axis=0), rolled[0:SUBLANE])
    return jnp.concatenate([top, rolled[SUBLANE:]], axis=0)


def _ssd_kernel(z_ref, x_ref, b_ref, c_ref, dt_ref, wx_ref, wb_ref, wc_ref, bx_ref, bb_ref, bc_ref,
                dtb_ref, alog_ref, dsk_ref, e_ref, nw_ref, o_ref, tail, state, dt_sp, acs_sp):
    q = CHUNK
    lb = x_ref.shape[0]
    gw = GROUP_W
    ns = SSM_STATE
    g = pl.program_id(2)
    n_chunks = lb // q

    @pl.when(pl.program_id(1) == 0)
    def _():
        tail[g] = jnp.zeros(tail.shape[1:], F32)
        state[g] = jnp.zeros(state.shape[1:], F32)

    @pl.when(g == 0)
    def _():
        dt_raw = dt_ref[...] + dtb_ref[...]
        dt = jnp.maximum(dt_raw, 0.0) + jnp.log(1.0 + jnp.exp(-jnp.abs(dt_raw)))
        a2 = dt * (-jnp.exp(alog_ref[...]) * math.log2(math.e))
        tri = (lax.broadcasted_iota(jnp.int32, (q, q), 0)
               >= lax.broadcasted_iota(jnp.int32, (q, q), 1)).astype(BF16)
        for i, piece in enumerate(_split3(dt)):
            dt_sp[i] = piece
        for c in range(n_chunks):
            acs = sum(_dot(tri, piece) for piece in _split3(a2[c * q:(c + 1) * q]))
            for i, piece in enumerate(_split3(acs)):
                acs_sp[i, c * q:(c + 1) * q, :] = piece

    e = e_ref[0]
    dt_e = sum(_dot(dt_sp[i], e) for i in range(3))
    acs_e = sum(_dot(acs_sp[i], e) for i in range(3))
    d_e = sum(_dot(jnp.broadcast_to(piece, (2 * SUBLANE, LANE)), e) for piece in _split3(dsk_ref[...]))[0:1]
    nw = nw_ref[...]

    xin = jnp.concatenate([x_ref[...], b_ref[...], c_ref[...]], axis=1).astype(F32)
    w = jnp.concatenate([wx_ref[...], wb_ref[...], wc_ref[...]], axis=1)
    conv = jnp.concatenate([bx_ref[...], bb_ref[...], bc_ref[...]], axis=1)
    prev_tail = tail[g]
    conv = conv + w[SSM_CONV - 1:SSM_CONV, :] * xin
    for k in range(1, SSM_CONV):
        conv = conv + w[SSM_CONV - 1 - k:SSM_CONV - k, :] * _shift_rows(xin, prev_tail, k)
    tail[g] = xin[lb - SUBLANE:lb, :]
    xc = conv * _sigmoid(conv)
    xs = xc[:, 0:gw]
    bm = xc[:, gw:gw + ns]
    cm = xc[:, gw + ns:gw + 2 * ns]

    li = lax.broadcasted_iota(jnp.int32, (q, gw), 0)
    si = lax.broadcasted_iota(jnp.int32, (q, gw), 1) & (q - 1)
    diag = li == si
    causal = li >= si
    br = lax.broadcasted_iota(jnp.int32, (gw, gw), 0) // SSM_HEAD_DIM
    bc = lax.broadcasted_iota(jnp.int32, (gw, gw), 1) // SSM_HEAD_DIM
    blockdiag = br == bc

    st = state[g]
    for c in range(n_chunks):
        r0 = c * q
        xs_c = xs[r0:r0 + q]
        bm_c = bm[r0:r0 + q].astype(BF16)
        cm_c = cm[r0:r0 + q].astype(BF16)
        acs = acs_e[r0:r0 + q]
        last = acs[q - 1:q, :]
        arow = jnp.sum(jnp.where(diag, acs, 0.0), axis=0, keepdims=True)
        decay = jnp.exp2(jnp.where(causal, acs - arow, -jnp.inf))
        b4 = jnp.concatenate([bm_c] * SSM_HPG, axis=0)
        cb = lax.dot_general(cm_c, b4, (((1,), (1,)), ((), ())), preferred_element_type=F32)
        m = (cb * decay).astype(BF16)
        xd = xs_c * dt_e[r0:r0 + q]
        xd_b = xd.astype(BF16)
        xbd = jnp.where(blockdiag, jnp.concatenate([xd_b] * SSM_HPG, axis=0), jnp.zeros((), BF16))
        y = _dot(m, xbd)
        y = y + _dot(cm_c, st.astype(BF16)) * jnp.exp2(acs)
        xds = (xd * jnp.exp2(last - acs)).astype(BF16)
        local = lax.dot_general(bm_c, xds, (((0,), (0,)), ((), ())), preferred_element_type=F32)
        st = jnp.exp2(last) * st + local
        y = y + d_e * xs_c
        z_c = z_ref[r0:r0 + q, :].astype(F32)
        gz = y * (z_c * _sigmoid(z_c))
        ms = jnp.mean(gz * gz, axis=-1, keepdims=True)
        o_ref[r0:r0 + q, :] = (gz * lax.rsqrt(ms + EPS) * nw).astype(o_ref.dtype)
    state[g] = st


def _ssd(zx, dt, conv_w, conv_b, dt_bias, a_log, d_skip, norm_w, seq):
    t = zx.shape[0]
    gw, ns, g = GROUP_W, SSM_STATE, SSM_GROUPS
    d_inner = g * gw
    lb = min(2048, seq)
    nj = seq // lb
    bsz = t // seq
    heads = dt_bias.shape[0]

    def pad_lanes(v):
        return jnp.zeros((1, LANE), F32).at[0, :heads].set(v)

    wpad = jnp.zeros((SUBLANE, conv_w.shape[1]), F32).at[:SSM_CONV].set(conv_w)
    cb2 = conv_b.reshape(1, -1)
    hh = jnp.arange(LANE)[None, :, None]
    cc = jnp.arange(gw)[None, None, :] // SSM_HEAD_DIM
    gg = jnp.arange(g)[:, None, None]
    expand = (hh == gg * SSM_HPG + cc).astype(BF16)

    row = lambda b, j, gi: b * nj + j
    xoff, boff, coff = d_inner // gw, (2 * d_inner) // ns, (2 * d_inner + g * ns) // ns
    wb_off, wc_off = d_inner // ns, (d_inner + g * ns) // ns
    in_specs = [
        pl.BlockSpec((lb, gw), lambda b, j, gi: (row(b, j, gi), gi)),
        pl.BlockSpec((lb, gw), lambda b, j, gi: (row(b, j, gi), xoff + gi)),
        pl.BlockSpec((lb, ns), lambda b, j, gi: (row(b, j, gi), boff + gi)),
        pl.BlockSpec((lb, ns), lambda b, j, gi: (row(b, j, gi), coff + gi)),
        pl.BlockSpec((lb, LANE), lambda b, j, gi: (row(b, j, gi), 0)),
        pl.BlockSpec((SUBLANE, gw), lambda b, j, gi: (0, gi)),
        pl.BlockSpec((SUBLANE, ns), lambda b, j, gi: (0, wb_off + gi)),
        pl.BlockSpec((SUBLANE, ns), lambda b, j, gi: (0, wc_off + gi)),
        pl.BlockSpec((1, gw), lambda b, j, gi: (0, gi)),
        pl.BlockSpec((1, ns), lambda b, j, gi: (0, wb_off + gi)),
        pl.BlockSpec((1, ns), lambda b, j, gi: (0, wc_off + gi)),
        pl.BlockSpec((1, LANE), lambda b, j, gi: (0, 0)),
        pl.BlockSpec((1, LANE), lambda b, j, gi: (0, 0)),
        pl.BlockSpec((1, LANE), lambda b, j, gi: (0, 0)),
        pl.BlockSpec((1, LANE, gw), lambda b, j, gi: (gi, 0, 0)),
        pl.BlockSpec((1, gw), lambda b, j, gi: (0, gi)),
    ]
    return pl.pallas_call(
        _ssd_kernel,
        grid=(bsz, nj, g),
        in_specs=in_specs,
        out_specs=pl.BlockSpec((lb, gw), lambda b, j, gi: (row(b, j, gi), gi)),
        out_shape=jax.ShapeDtypeStruct((t, d_inner), BF16),
        scratch_shapes=[
            pltpu.VMEM((g, SUBLANE, gw + 2 * ns), F32),
            pltpu.VMEM((g, ns, gw), F32),
            pltpu.VMEM((3, lb, LANE), BF16),
            pltpu.VMEM((3, lb, LANE), BF16),
        ],
        compiler_params=_cparams(("parallel", "arbitrary", "arbitrary")),
        name="ssd",
    )(zx, zx, zx, zx, dt, wpad, wpad, wpad, cb2, cb2, cb2,
      pad_lanes(dt_bias), pad_lanes(a_log), pad_lanes(d_skip), expand, norm_w.reshape(1, -1))


def _attn_kernel(q_ref, k_ref, v_ref, lamp_ref, subw_ref, o_ref, vt_ref, acc_ref, m_ref,
                 sa_ref, bma_ref, sb_ref, bmb_ref,
                 *, lambda_init):
    tq = q_ref.shape[1]
    tk = vt_ref.shape[2]
    seq = k_ref.shape[1]
    dh = DIFF_HEAD_DIM
    dv = DIFF_V_DIM
    qi = pl.program_id(2)

    @pl.when(qi == 0)
    def _():
        for i in range(seq // tk):
            vt_ref[i, 0:dv, :] = v_ref[0, i * tk:(i + 1) * tk, :].astype(F32).T.astype(BF16)
            vt_ref[i, dv:, :] = jnp.ones((vt_ref.shape[1] - dv, tk), BF16)

    qb = q_ref[0]
    lane = lax.broadcasted_iota(jnp.int32, qb.shape, 1)
    zero = jnp.zeros((), BF16)
    qs = (jnp.where(lane < dh, qb, zero), jnp.where(lane >= dh, qb, zero))

    m_ref[...] = jnp.full(m_ref.shape, -jnp.inf, F32)
    acc_ref[...] = jnp.zeros(acc_ref.shape, F32)

    def scores(kj, s_ref, bm_ref, cols, diag_off=None):
        c0, c1 = cols
        r0 = pl.multiple_of(kj * tk, tk)
        kb = k_ref[0, pl.ds(r0, tk), :]
        for i in range(2):
            s = lax.dot_general(kb, qs[i][c0:c1], (((1,), (1,)), ((), ())), preferred_element_type=F32)
            if diag_off is not None:
                kc = lax.broadcasted_iota(jnp.int32, s.shape, 0) // CHUNK + diag_off
                qc = (lax.broadcasted_iota(jnp.int32, s.shape, 1) + c0) // CHUNK
                s = jnp.where(kc <= qc, s, -jnp.inf)
            s_ref[i, :, c0:c1] = s
            bm_ref[i, :, c0:c1] = jnp.max(s, axis=0, keepdims=True)

    def update(kj, s_ref, bm_ref, cols):
        c0, c1 = cols
        vtb = vt_ref[kj]
        for i in range(2):
            m_old = m_ref[i, :, c0:c1]
            m_new = jnp.maximum(m_old, bm_ref[i, :, c0:c1])
            alpha = jnp.exp2(m_old - m_new)
            p = jnp.exp2(s_ref[i, :, c0:c1] - m_new)
            acc_ref[i, :, c0:c1] = alpha * acc_ref[i, :, c0:c1] + _dot(vtb, p.astype(BF16))
            m_ref[i, :, c0:c1] = m_new

    def score_pieces(kj, buf, diag_off=None, lo=0):
        return [functools.partial(scores, kj, *buf, (t, t + ATTN_COL_TILE), diag_off)
                for t in range(lo, tq, ATTN_COL_TILE)]

    def update_pieces(kj, buf, lo=0):
        return [functools.partial(update, kj, *buf, (t, t + ATTN_COL_TILE)) for t in range(lo, tq, ATTN_COL_TILE)]

    def stage(*calls):
        for group in itertools.zip_longest(*calls):
            for call in group:
                if call is not None:
                    call()

    a_buf, b_buf = (sa_ref, bma_ref), (sb_ref, bmb_ref)
    d0 = 2 * qi
    off1 = tk // CHUNK

    @pl.when(qi == 0)
    def _():
        stage(score_pieces(0, a_buf, 0))
        stage(score_pieces(1, b_buf, off1, tk), update_pieces(0, a_buf))
        stage(update_pieces(1, b_buf, tk))

    @pl.when(qi > 0)
    def _():
        stage(score_pieces(0, a_buf))

        def body(u, carry):
            stage(score_pieces(2 * u + 1, b_buf), update_pieces(2 * u, a_buf))
            stage(score_pieces(2 * u + 2, a_buf), update_pieces(2 * u + 1, b_buf))
            return carry

        lax.fori_loop(0, qi - 1, body, 0)
        stage(score_pieces(d0 - 1, b_buf), update_pieces(d0 - 2, a_buf))
        stage(score_pieces(d0, a_buf, 0), update_pieces(d0 - 1, b_buf))
        stage(score_pieces(d0 + 1, b_buf, off1, tk), update_pieces(d0, a_buf))
        stage(update_pieces(d0 + 1, b_buf, tk))

    lp = lamp_ref[...]
    lam = (jnp.exp(jnp.sum(lp[0:1] * lp[1:2], axis=-1, keepdims=True))
           - jnp.exp(jnp.sum(lp[2:3] * lp[3:4], axis=-1, keepdims=True)) + lambda_init)
    o = (acc_ref[0, 0:dv, :] / acc_ref[0, dv:dv + 1, :]
         - lam * (acc_ref[1, 0:dv, :] / acc_ref[1, dv:dv + 1, :]))
    ms = jnp.mean(o * o, axis=0, keepdims=True)
    sw = jnp.concatenate([subw_ref[...]] * (tq // LANE), axis=1)
    o = o * lax.rsqrt(ms + EPS) * sw * (1.0 - lambda_init)
    o_ref[0] = o.T.astype(o_ref.dtype)


def _attn(qkv, lq1, lk1, lq2, lk2, subln_w, lambda_init, bsz, seq):
    d3 = qkv.shape[1]
    d = d3 // 3
    heads = d // (2 * DIFF_HEAD_DIM)
    tk = min(512, seq // 2)
    tq = 2 * tk
    assert tk % CHUNK == 0 and seq % tq == 0
    qkv3 = qkv.reshape(bsz, seq, d3)
    lamp = jnp.zeros((SUBLANE, LANE), F32)
    for r, v in enumerate((lq1, lk1, lq2, lk2)):
        lamp = lamp.at[r, :v.shape[0]].set(v)
    subw = jnp.broadcast_to(subln_w[:, None], (DIFF_V_DIM, LANE)).astype(F32)
    out = pl.pallas_call(
        functools.partial(_attn_kernel, lambda_init=lambda_init),
        grid=(bsz, heads, seq // tq),
        in_specs=[
            pl.BlockSpec((1, tq, LANE), lambda b, h, i: (b, i, h)),
            pl.BlockSpec((1, seq, LANE), lambda b, h, i: (b, 0, heads + h)),
            pl.BlockSpec((1, seq, LANE), lambda b, h, i: (b, 0, 2 * heads + h)),
            pl.BlockSpec((SUBLANE, LANE), lambda b, h, i: (0, 0)),
            pl.BlockSpec((DIFF_V_DIM, LANE), lambda b, h, i: (0, 0)),
        ],
        out_specs=pl.BlockSpec((1, tq, LANE), lambda b, h, i: (b, i, h)),
        out_shape=jax.ShapeDtypeStruct((bsz, seq, d), BF16),
        scratch_shapes=[
            pltpu.VMEM((seq // tk, DIFF_V_DIM + ATTN_ONES_ROWS, tk), BF16),
            pltpu.VMEM((2, DIFF_V_DIM + ATTN_ONES_ROWS, tq), F32),
            pltpu.VMEM((2, 1, tq), F32),
            pltpu.VMEM((2, tk, tq), F32), pltpu.VMEM((2, 1, tq), F32),
            pltpu.VMEM((2, tk, tq), F32), pltpu.VMEM((2, 1, tq), F32),
        ],
        compiler_params=_cparams(("parallel", "parallel", "arbitrary")),
        name="attn",
    )(qkv3, qkv3, qkv3, lamp, subw)
    return out.reshape(bsz * seq, d)


def kernel(x, c, ada_w, ada_b, ln1_g, ln1_b, ln2_g, ln2_b, mlp_w1, mlp_w2, ssm_w_in, ssm_conv_w, ssm_conv_b, ssm_dt_bias, ssm_a_log, ssm_d, ssm_norm_w, ssm_w_out, attn_w_qkv, attn_lq1, attn_lk1, attn_lq2, attn_lk2, attn_subln_w, attn_w_out):
    bsz, seq, d = x.shape
    depth = ada_w.shape[0]
    mods = _mods(c, ada_w, ada_b)
    x2 = x.reshape(bsz * seq, d)
    n_main = SSM_GROUPS * GROUP_W * 2 + 2 * SSM_GROUPS * SSM_STATE
    for i in range(depth):
        j = i // 2
        if i % 2 == 0:
            w_in = ssm_w_in[j]
            heads = w_in.shape[1] - n_main
            w_dt = jnp.zeros((d, LANE), F32).at[:, :heads].set(w_in[:, n_main:])
            zx, dt = _proj(x2, mods[i], 1, 0, w_in[:, :n_main].astype(BF16), BF16, seq, w_dt=w_dt.astype(BF16))
            a = _ssd(zx, dt, ssm_conv_w[j], ssm_conv_b[j], ssm_dt_bias[j], ssm_a_log[j], ssm_d[j],
                     ssm_norm_w[j], seq)
            w_out = ssm_w_out[j]
        else:
            lambda_init = 0.8 - 0.6 * math.exp(-0.3 * i)
            col_scale = jnp.concatenate([jnp.full((d,), math.log2(math.e) * DIFF_HEAD_DIM ** -0.5, F32),
                                         jnp.ones((2 * d,), F32)])
            qkv = _proj(x2, mods[i], 1, 0, attn_w_qkv[j].astype(BF16), BF16, seq, col_scale=col_scale)
            a = _attn(qkv, attn_lq1[j], attn_lk1[j], attn_lq2[j], attn_lk2[j], attn_subln_w[j],
                      lambda_init, bsz, seq)
            w_out = attn_w_out[j]
        x2 = _outproj(a, w_out.astype(BF16), x2, mods[i], 2, ln1_g[i], ln1_b[i], seq)
        x2 = _mlp(x2, mods[i], mlp_w1[i].astype(BF16), mlp_w2[i].astype(BF16), ln2_g[i], ln2_b[i], seq)
    return x2.reshape(bsz, seq, d)
```

```python
import functools
import itertools
import math

import jax
import jax.numpy as jnp
from jax import lax
from jax.experimental import pallas as pl
from jax.experimental.pallas import tpu as pltpu

F32 = jnp.float32
BF16 = jnp.bfloat16

DEPTH = 2
CHUNK = 64
SSM_HEAD_DIM = 64
SSM_GROUPS = 8
SSM_HPG = 4
SSM_STATE = 128
SSM_CONV = 4
GROUP_W = SSM_HPG * SSM_HEAD_DIM
DIFF_HEAD_DIM = 64
DIFF_V_DIM = 128
ATTN_COL_TILE = 256
ATTN_ONES_ROWS = 16
DEEPNORM_ALPHA = (2 * DEPTH) ** 0.25
EPS = 1e-5

LANE = 128
SUBLANE = 8
VMEM_LIMIT = 56 * 1024 * 1024
MLP_TM, MLP_NSUB, MLP_TF = 1024, 4, 1024
OUTPROJ_TM, OUTPROJ_NSUB = 1024, 4
PROJ_TM, PROJ_NSUB, PROJ_TN = 1024, 2, 1024


def _cparams(sem):
    return pltpu.CompilerParams(dimension_semantics=sem, vmem_limit_bytes=VMEM_LIMIT)


def _sigmoid(x):
    return 1.0 / (1.0 + jnp.exp(-x))


def _layer_norm(v, g, b):
    mu = jnp.mean(v, axis=-1, keepdims=True)
    vc = v - mu
    var = jnp.mean(vc * vc, axis=-1, keepdims=True)
    return vc * lax.rsqrt(var + EPS) * g + b


def _dot(a, b):
    return jnp.dot(a, b, preferred_element_type=F32)


def _mods_kernel(c_ref, w_ref, b_ref, o_ref):
    c = c_ref[...]
    cond = c * _sigmoid(c)
    o_ref[0] = _dot(cond.astype(BF16), w_ref[0].astype(BF16)) + b_ref[0]


def _mods(c, ada_w, ada_b):
    depth, d, n = ada_w.shape
    bsz = c.shape[0]
    cp = jnp.zeros((SUBLANE, d), F32).at[:bsz].set(c)
    tn = 1024
    out = pl.pallas_call(
        _mods_kernel,
        grid=(depth, n // tn),
        in_specs=[
            pl.BlockSpec((SUBLANE, d), lambda l, j: (0, 0)),
            pl.BlockSpec((1, d, tn), lambda l, j: (l, 0, j)),
            pl.BlockSpec((1, 1, tn), lambda l, j: (l, 0, j)),
        ],
        out_specs=pl.BlockSpec((1, SUBLANE, tn), lambda l, j: (l, 0, j)),
        out_shape=jax.ShapeDtypeStruct((depth, SUBLANE, n), F32),
        compiler_params=_cparams(("parallel", "parallel")),
        name="mods",
    )(cp, ada_w, ada_b.reshape(depth, 1, n))
    return out[:, :bsz].reshape(depth, bsz * 6, 1, d)


def _proj_body(x_ref, sc_ref, sh_ref, w_ref, o_ref, n_sub, tn, cs_ref=None, wdt_ref=None, dt_ref=None):
    rows = x_ref.shape[0] // n_sub
    n = w_ref.shape[1]
    for s in range(n_sub):
        r0 = s * rows
        hb = (x_ref[r0:r0 + rows, :] * (1.0 + sc_ref[0]) + sh_ref[0]).astype(BF16)
        for j in range(n // tn):
            y = _dot(hb, w_ref[:, j * tn:(j + 1) * tn])
            if cs_ref is not None:
                y = y * cs_ref[:, j * tn:(j + 1) * tn]
            o_ref[r0:r0 + rows, j * tn:(j + 1) * tn] = y.astype(o_ref.dtype)
        if dt_ref is not None:
            dt_ref[r0:r0 + rows, :] = _dot(hb, wdt_ref[...])


def _proj_scaled_kernel(x_ref, sc_ref, sh_ref, w_ref, cs_ref, o_ref, *, n_sub, tn):
    _proj_body(x_ref, sc_ref, sh_ref, w_ref, o_ref, n_sub, tn, cs_ref=cs_ref)


def _proj_dt_kernel(x_ref, sc_ref, sh_ref, w_ref, wdt_ref, o_ref, dt_ref, *, n_sub, tn):
    _proj_body(x_ref, sc_ref, sh_ref, w_ref, o_ref, n_sub, tn, wdt_ref=wdt_ref, dt_ref=dt_ref)


def _proj(x2, mods, k_scale, k_shift, w, layer, n, out_dtype, seq, w_dt=None, col_scale=None):
    t, d = x2.shape
    tm = min(PROJ_TM, seq)
    per_seq = seq // tm
    resident = dict(pipeline_mode=pl.Buffered(1))
    in_specs = [
        pl.BlockSpec((tm, d), lambda i: (i, 0)),
        pl.BlockSpec((1, 1, d), lambda i: ((i // per_seq) * 6 + k_scale, 0, 0)),
        pl.BlockSpec((1, 1, d), lambda i: ((i // per_seq) * 6 + k_shift, 0, 0)),
        pl.BlockSpec((None, d, n), lambda i: (layer, 0, 0), **resident),
    ]
    out_spec = pl.BlockSpec((tm, n), lambda i: (i, 0))
    out_shape = jax.ShapeDtypeStruct((t, n), out_dtype)
    static = dict(n_sub=PROJ_NSUB, tn=min(PROJ_TN, n))
    cp = _cparams(("parallel",))
    if w_dt is None:
        return pl.pallas_call(
            functools.partial(_proj_scaled_kernel, **static),
            grid=(t // tm,),
            in_specs=in_specs + [pl.BlockSpec((1, n), lambda i: (0, 0))],
            out_specs=out_spec,
            out_shape=out_shape,
            compiler_params=cp,
            name="proj",
        )(x2, mods, mods, w, col_scale.reshape(1, n))
    ndt = w_dt.shape[1]
    return pl.pallas_call(
        functools.partial(_proj_dt_kernel, **static),
        grid=(t // tm,),
        in_specs=in_specs + [pl.BlockSpec((d, ndt), lambda i: (0, 0), **resident)],
        out_specs=[out_spec, pl.BlockSpec((tm, ndt), lambda i: (i, 0))],
        out_shape=[out_shape, jax.ShapeDtypeStruct((t, ndt), F32)],
        compiler_params=cp,
        name="proj_dt",
    )(x2, mods, mods, w, w_dt)


def _outproj_kernel(a_ref, w_ref, x_ref, g_ref, lng_ref, lnb_ref, o_ref, *, n_sub):
    rows = x_ref.shape[0] // n_sub
    for s in range(n_sub):
        r0 = s * rows
        y = _dot(a_ref[r0:r0 + rows, :], w_ref[...])
        v = DEEPNORM_ALPHA * x_ref[r0:r0 + rows, :] + g_ref[0] * y
        o_ref[r0:r0 + rows, :] = _layer_norm(v, lng_ref[...], lnb_ref[...])


def _outproj(a, w, layer, x2, mods, k_gate, ln_g, ln_b, seq):
    t, d = x2.shape
    k = a.shape[1]
    tm = min(OUTPROJ_TM, seq)
    per_seq = seq // tm
    return pl.pallas_call(
        functools.partial(_outproj_kernel, n_sub=OUTPROJ_NSUB),
        grid=(t // tm,),
        in_specs=[
            pl.BlockSpec((tm, k), lambda i: (i, 0)),
            pl.BlockSpec((None, k, d), lambda i: (layer, 0, 0), pipeline_mode=pl.Buffered(1)),
            pl.BlockSpec((tm, d), lambda i: (i, 0)),
            pl.BlockSpec((1, 1, d), lambda i: ((i // per_seq) * 6 + k_gate, 0, 0)),
            pl.BlockSpec((1, d), lambda i: (0, 0)),
            pl.BlockSpec((1, d), lambda i: (0, 0)),
        ],
        out_specs=pl.BlockSpec((tm, d), lambda i: (i, 0)),
        out_shape=jax.ShapeDtypeStruct((t, d), F32),
        compiler_params=_cparams(("parallel",)),
        name="outproj",
    )(a, w, x2, mods, ln_g.reshape(1, d), ln_b.reshape(1, d))


def _mlp_kernel(x_ref, sc_ref, sh_ref, g_ref, w1_ref, w2_ref, lng_ref, lnb_ref, o_ref, *, n_sub, tf):
    tm = x_ref.shape[0]
    ff = w1_ref.shape[1]
    rows = tm // n_sub
    for s in range(n_sub):
        r0 = s * rows
        x = x_ref[r0:r0 + rows, :]
        hb = (x * (1.0 + sc_ref[0]) + sh_ref[0]).astype(BF16)
        acc = None
        for f in range(ff // tf):
            u = jnp.maximum(_dot(hb, w1_ref[:, f * tf:(f + 1) * tf]), 0.0)
            part = _dot((u * u).astype(BF16), w2_ref[f * tf:(f + 1) * tf, :])
            acc = part if acc is None else acc + part
        v = DEEPNORM_ALPHA * x + g_ref[0] * acc
        o_ref[r0:r0 + rows, :] = _layer_norm(v, lng_ref[...], lnb_ref[...])


def _mlp(x2, mods, w1, w2, layer, ln_g, ln_b, seq):
    t, d = x2.shape
    ff = w1.shape[2]
    tm = min(MLP_TM, seq)
    per_seq = seq // tm

    def mod_spec(k):
        return pl.BlockSpec((1, 1, d), lambda i: ((i // per_seq) * 6 + k, 0, 0))

    resident = dict(pipeline_mode=pl.Buffered(1))
    return pl.pallas_call(
        functools.partial(_mlp_kernel, n_sub=MLP_NSUB, tf=min(MLP_TF, ff)),
        grid=(t // tm,),
        in_specs=[
            pl.BlockSpec((tm, d), lambda i: (i, 0)),
            mod_spec(4), mod_spec(3), mod_spec(5),
            pl.BlockSpec((None, d, ff), lambda i: (layer, 0, 0), **resident),
            pl.BlockSpec((None, ff, d), lambda i: (layer, 0, 0), **resident),
            pl.BlockSpec((1, d), lambda i: (0, 0)),
            pl.BlockSpec((1, d), lambda i: (0, 0)),
        ],
        out_specs=pl.BlockSpec((tm, d), lambda i: (i, 0)),
        out_shape=jax.ShapeDtypeStruct((t, d), F32),
        compiler_params=_cparams(("parallel",)),
        name="mlp",
    )(x2, mods, mods, mods, w1, w2, ln_g.reshape(1, d), ln_b.reshape(1, d))


def _split3(x):
    hi = x.astype(BF16)
    r1 = x - hi.astype(F32)
    mid = r1.astype(BF16)
    lo = (r1 - mid.astype(F32)).astype(BF16)
    return hi, mid, lo


def _shift_rows(x, prev_tail, k):
    rolled = pltpu.roll(x, k, axis=0)
    top = jnp.where(lax.broadcasted_iota(jnp.int32, prev_tail.shape, 0) < k,
                    pltpu.roll(prev_tail, k, axis=0), rolled[0:SUBLANE])
    return jnp.concatenate([top, rolled[SUBLANE:]], axis=0)


def _ssd_kernel(z_ref, x_ref, b_ref, c_ref, dt_ref, wx_ref, wb_ref, wc_ref, bx_ref, bb_ref, bc_ref,
                dtb_ref, alog_ref, dsk_ref, e_ref, nw_ref, o_ref, tail, state, dt_sp, acs_sp):
    q = CHUNK
    lb = x_ref.shape[0]
    gw = GROUP_W
    ns = SSM_STATE
    g = pl.program_id(2)
    n_chunks = lb // q

    @pl.when(pl.program_id(1) == 0)
    def _():
        tail[g] = jnp.zeros(tail.shape[1:], F32)
        state[g] = jnp.zeros(state.shape[1:], F32)

    @pl.when(g == 0)
    def _():
        dt_raw = dt_ref[...] + dtb_ref[...]
        dt = jnp.maximum(dt_raw, 0.0) + jnp.log(1.0 + jnp.exp(-jnp.abs(dt_raw)))
        a2 = dt * (-jnp.exp(alog_ref[...]) * math.log2(math.e))
        tri = (lax.broadcasted_iota(jnp.int32, (q, q), 0)
               >= lax.broadcasted_iota(jnp.int32, (q, q), 1)).astype(BF16)
        for i, piece in enumerate(_split3(dt)):
            dt_sp[i] = piece
        for c in range(n_chunks):
            acs = sum(_dot(tri, piece) for piece in _split3(a2[c * q:(c + 1) * q]))
            for i, piece in enumerate(_split3(acs)):
                acs_sp[i, c * q:(c + 1) * q, :] = piece

    e = e_ref[0]
    dt_e = sum(_dot(dt_sp[i], e) for i in range(3))
    acs_e = sum(_dot(acs_sp[i], e) for i in range(3))
    d_e = sum(_dot(jnp.broadcast_to(piece, (2 * SUBLANE, LANE)), e) for piece in _split3(dsk_ref[...]))[0:1]
    nw = nw_ref[...]

    xin = jnp.concatenate([x_ref[...], b_ref[...], c_ref[...]], axis=1).astype(F32)
    w = jnp.concatenate([wx_ref[...], wb_ref[...], wc_ref[...]], axis=1)
    conv = jnp.concatenate([bx_ref[...], bb_ref[...], bc_ref[...]], axis=1)
    prev_tail = tail[g]
    conv = conv + w[SSM_CONV - 1:SSM_CONV, :] * xin
    for k in range(1, SSM_CONV):
        conv = conv + w[SSM_CONV - 1 - k:SSM_CONV - k, :] * _shift_rows(xin, prev_tail, k)
    tail[g] = xin[lb - SUBLANE:lb, :]
    xc = conv * _sigmoid(conv)
    xs = xc[:, 0:gw]
    bm = xc[:, gw:gw + ns]
    cm = xc[:, gw + ns:gw + 2 * ns]

    li = lax.broadcasted_iota(jnp.int32, (q, gw), 0)
    si = lax.broadcasted_iota(jnp.int32, (q, gw), 1) & (q - 1)
    diag = li == si
    causal = li >= si
    br = lax.broadcasted_iota(jnp.int32, (gw, gw), 0) // SSM_HEAD_DIM
    bc = lax.broadcasted_iota(jnp.int32, (gw, gw), 1) // SSM_HEAD_DIM
    blockdiag = br == bc

    st = state[g]
    for c in range(n_chunks):
        r0 = c * q
        xs_c = xs[r0:r0 + q]
        bm_c = bm[r0:r0 + q].astype(BF16)
        cm_c = cm[r0:r0 + q].astype(BF16)
        acs = acs_e[r0:r0 + q]
        last = acs[q - 1:q, :]
        arow = jnp.sum(jnp.where(diag, acs, 0.0), axis=0, keepdims=True)
        decay = jnp.exp2(jnp.where(causal, acs - arow, -jnp.inf))
        b4 = jnp.concatenate([bm_c] * SSM_HPG, axis=0)
        cb = lax.dot_general(cm_c, b4, (((1,), (1,)), ((), ())), preferred_element_type=F32)
        m = (cb * decay).astype(BF16)
        xd = xs_c * dt_e[r0:r0 + q]
        xd_b = xd.astype(BF16)
        xbd = jnp.where(blockdiag, jnp.concatenate([xd_b] * SSM_HPG, axis=0), jnp.zeros((), BF16))
        y = _dot(m, xbd)
        y = y + _dot(cm_c, st.astype(BF16)) * jnp.exp2(acs)
        xds = (xd * jnp.exp2(last - acs)).astype(BF16)
        local = lax.dot_general(bm_c, xds, (((0,), (0,)), ((), ())), preferred_element_type=F32)
        st = jnp.exp2(last) * st + local
        y = y + d_e * xs_c
        z_c = z_ref[r0:r0 + q, :].astype(F32)
        gz = y * (z_c * _sigmoid(z_c))
        ms = jnp.mean(gz * gz, axis=-1, keepdims=True)
        o_ref[r0:r0 + q, :] = (gz * lax.rsqrt(ms + EPS) * nw).astype(o_ref.dtype)
    state[g] = st


def _ssd(zx, dt, conv_w, conv_b, dt_bias, a_log, d_skip, norm_w, seq):
    t = zx.shape[0]
    gw, ns, g = GROUP_W, SSM_STATE, SSM_GROUPS
    d_inner = g * gw
    lb = min(2048, seq)
    nj = seq // lb
    bsz = t // seq
    heads = dt_bias.shape[0]

    def pad_lanes(v):
        return jnp.zeros((1, LANE), F32).at[0, :heads].set(v)

    wpad = jnp.zeros((SUBLANE, conv_w.shape[1]), F32).at[:SSM_CONV].set(conv_w)
    cb2 = conv_b.reshape(1, -1)
    hh = jnp.arange(LANE)[None, :, None]
    cc = jnp.arange(gw)[None, None, :] // SSM_HEAD_DIM
    gg = jnp.arange(g)[:, None, None]
    expand = (hh == gg * SSM_HPG + cc).astype(BF16)

    row = lambda b, j, gi: b * nj + j
    xoff, boff, coff = d_inner // gw, (2 * d_inner) // ns, (2 * d_inner + g * ns) // ns
    wb_off, wc_off = d_inner // ns, (d_inner + g * ns) // ns
    in_specs = [
        pl.BlockSpec((lb, gw), lambda b, j, gi: (row(b, j, gi), gi)),
        pl.BlockSpec((lb, gw), lambda b, j, gi: (row(b, j, gi), xoff + gi)),
        pl.BlockSpec((lb, ns), lambda b, j, gi: (row(b, j, gi), boff + gi)),
        pl.BlockSpec((lb, ns), lambda b, j, gi: (row(b, j, gi), coff + gi)),
        pl.BlockSpec((lb, LANE), lambda b, j, gi: (row(b, j, gi), 0)),
        pl.BlockSpec((SUBLANE, gw), lambda b, j, gi: (0, gi)),
        pl.BlockSpec((SUBLANE, ns), lambda b, j, gi: (0, wb_off + gi)),
        pl.BlockSpec((SUBLANE, ns), lambda b, j, gi: (0, wc_off + gi)),
        pl.BlockSpec((1, gw), lambda b, j, gi: (0, gi)),
        pl.BlockSpec((1, ns), lambda b, j, gi: (0, wb_off + gi)),
        pl.BlockSpec((1, ns), lambda b, j, gi: (0, wc_off + gi)),
        pl.BlockSpec((1, LANE), lambda b, j, gi: (0, 0)),
        pl.BlockSpec((1, LANE), lambda b, j, gi: (0, 0)),
        pl.BlockSpec((1, LANE), lambda b, j, gi: (0, 0)),
        pl.BlockSpec((1, LANE, gw), lambda b, j, gi: (gi, 0, 0)),
        pl.BlockSpec((1, gw), lambda b, j, gi: (0, gi)),
    ]
    return pl.pallas_call(
        _ssd_kernel,
        grid=(bsz, nj, g),
        in_specs=in_specs,
        out_specs=pl.BlockSpec((lb, gw), lambda b, j, gi: (row(b, j, gi), gi)),
        out_shape=jax.ShapeDtypeStruct((t, d_inner), BF16),
        scratch_shapes=[
            pltpu.VMEM((g, SUBLANE, gw + 2 * ns), F32),
            pltpu.VMEM((g, ns, gw), F32),
            pltpu.VMEM((3, lb, LANE), BF16),
            pltpu.VMEM((3, lb, LANE), BF16),
        ],
        compiler_params=_cparams(("parallel", "arbitrary", "arbitrary")),
        name="ssd",
    )(zx, zx, zx, zx, dt, wpad, wpad, wpad, cb2, cb2, cb2,
      pad_lanes(dt_bias), pad_lanes(a_log), pad_lanes(d_skip), expand, norm_w.reshape(1, -1))


def _attn_kernel(q_ref, k_ref, v_ref, lamp_ref, subw_ref, o_ref, vt_ref, acc_ref, m_ref,
                 sa_ref, bma_ref, sb_ref, bmb_ref,
                 *, lambda_init):
    tq = q_ref.shape[1]
    tk = vt_ref.shape[2]
    seq = k_ref.shape[1]
    dh = DIFF_HEAD_DIM
    dv = DIFF_V_DIM
    qi = pl.program_id(2)

    @pl.when(qi == 0)
    def _():
        for i in range(seq // tk):
            vt_ref[i, 0:dv, :] = v_ref[0, i * tk:(i + 1) * tk, :].astype(F32).T.astype(BF16)
            vt_ref[i, dv:, :] = jnp.ones((vt_ref.shape[1] - dv, tk), BF16)

    qb = q_ref[0]
    lane = lax.broadcasted_iota(jnp.int32, qb.shape, 1)
    zero = jnp.zeros((), BF16)
    qs = (jnp.where(lane < dh, qb, zero), jnp.where(lane >= dh, qb, zero))

    m_ref[...] = jnp.full(m_ref.shape, -jnp.inf, F32)
    acc_ref[...] = jnp.zeros(acc_ref.shape, F32)

    def scores(kj, s_ref, bm_ref, cols, comp, diag_off=None):
        c0, c1 = cols
        r0 = pl.multiple_of(kj * tk, tk)
        kb = k_ref[0, pl.ds(r0, tk), :]
        s = lax.dot_general(kb, qs[comp][c0:c1], (((1,), (1,)), ((), ())), preferred_element_type=F32)
        if diag_off is not None:
            kc = lax.broadcasted_iota(jnp.int32, s.shape, 0) // CHUNK + diag_off
            qc = (lax.broadcasted_iota(jnp.int32, s.shape, 1) + c0) // CHUNK
            s = jnp.where(kc <= qc, s, -jnp.inf)
        s_ref[comp, :, c0:c1] = s
        bm_ref[comp, :, c0:c1] = jnp.max(s, axis=0, keepdims=True)

    def update(kj, s_ref, bm_ref, cols, comp):
        c0, c1 = cols
        vtb = vt_ref[kj]
        m_old = m_ref[comp, :, c0:c1]
        m_new = jnp.maximum(m_old, bm_ref[comp, :, c0:c1])
        alpha = jnp.exp2(m_old - m_new)
        p = jnp.exp2(s_ref[comp, :, c0:c1] - m_new)
        acc_ref[comp, :, c0:c1] = alpha * acc_ref[comp, :, c0:c1] + _dot(vtb, p.astype(BF16))
        m_ref[comp, :, c0:c1] = m_new

    def score_pieces(kj, buf, diag_off=None, lo=0):
        return [functools.partial(scores, kj, *buf, (t, t + ATTN_COL_TILE), comp, diag_off)
                for t in range(lo, tq, ATTN_COL_TILE) for comp in range(2)]

    def update_pieces(kj, buf, lo=0):
        return [functools.partial(update, kj, *buf, (t, t + ATTN_COL_TILE), comp)
                for t in range(lo, tq, ATTN_COL_TILE) for comp in range(2)]

    def stage(*calls):
        for group in itertools.zip_longest(*calls):
            for call in group:
                if call is not None:
                    call()

    a_buf, b_buf = (sa_ref, bma_ref), (sb_ref, bmb_ref)
    d0 = 2 * qi
    off1 = tk // CHUNK

    @pl.when(qi == 0)
    def _():
        stage(score_pieces(0, a_buf, 0))
        stage(score_pieces(1, b_buf, off1, tk), update_pieces(0, a_buf))
        stage(update_pieces(1, b_buf, tk))

    @pl.when(qi > 0)
    def _():
        stage(score_pieces(0, a_buf))

        def pair(u):
            stage(score_pieces(2 * u + 1, b_buf), update_pieces(2 * u, a_buf))
            stage(score_pieces(2 * u + 2, a_buf), update_pieces(2 * u + 1, b_buf))

        def body(v, carry):
            pair(2 * v)
            pair(2 * v + 1)
            return carry

        n_pairs = qi - 1
        lax.fori_loop(0, n_pairs // 2, body, 0)

        @pl.when(n_pairs % 2 == 1)
        def _():
            pair(n_pairs - 1)

        stage(score_pieces(d0 - 1, b_buf), update_pieces(d0 - 2, a_buf))
        stage(score_pieces(d0, a_buf, 0), update_pieces(d0 - 1, b_buf))
        stage(score_pieces(d0 + 1, b_buf, off1, tk), update_pieces(d0, a_buf))
        stage(update_pieces(d0 + 1, b_buf, tk))

    lp = lamp_ref[...]
    lam = (jnp.exp(jnp.sum(lp[0:1] * lp[1:2], axis=-1, keepdims=True))
           - jnp.exp(jnp.sum(lp[2:3] * lp[3:4], axis=-1, keepdims=True)) + lambda_init)
    o = (acc_ref[0, 0:dv, :] / acc_ref[0, dv:dv + 1, :]
         - lam * (acc_ref[1, 0:dv, :] / acc_ref[1, dv:dv + 1, :]))
    ms = jnp.mean(o * o, axis=0, keepdims=True)
    sw = jnp.concatenate([subw_ref[...]] * (tq // LANE), axis=1)
    o = o * lax.rsqrt(ms + EPS) * sw * (1.0 - lambda_init)
    o_ref[0] = o.T.astype(o_ref.dtype)


def _attn(qkv, lq1, lk1, lq2, lk2, subln_w, lambda_init, bsz, seq):
    d3 = qkv.shape[1]
    d = d3 // 3
    heads = d // (2 * DIFF_HEAD_DIM)
    tk = min(512, seq // 2)
    tq = 2 * tk
    assert tk % CHUNK == 0 and seq % tq == 0
    qkv3 = qkv.reshape(bsz, seq, d3)
    lamp = jnp.zeros((SUBLANE, LANE), F32)
    for r, v in enumerate((lq1, lk1, lq2, lk2)):
        lamp = lamp.at[r, :v.shape[0]].set(v)
    subw = jnp.broadcast_to(subln_w[:, None], (DIFF_V_DIM, LANE)).astype(F32)
    out = pl.pallas_call(
        functools.partial(_attn_kernel, lambda_init=lambda_init),
        grid=(bsz, heads, seq // tq),
        in_specs=[
            pl.BlockSpec((1, tq, LANE), lambda b, h, i: (b, i, h)),
            pl.BlockSpec((1, seq, LANE), lambda b, h, i: (b, 0, heads + h)),
            pl.BlockSpec((1, seq, LANE), lambda b, h, i: (b, 0, 2 * heads + h)),
            pl.BlockSpec((SUBLANE, LANE), lambda b, h, i: (0, 0)),
            pl.BlockSpec((DIFF_V_DIM, LANE), lambda b, h, i: (0, 0)),
        ],
        out_specs=pl.BlockSpec((1, tq, LANE), lambda b, h, i: (b, i, h)),
        out_shape=jax.ShapeDtypeStruct((bsz, seq, d), BF16),
        scratch_shapes=[
            pltpu.VMEM((seq // tk, DIFF_V_DIM + ATTN_ONES_ROWS, tk), BF16),
            pltpu.VMEM((2, DIFF_V_DIM + ATTN_ONES_ROWS, tq), F32),
            pltpu.VMEM((2, 1, tq), F32),
            pltpu.VMEM((2, tk, tq), F32), pltpu.VMEM((2, 1, tq), F32),
            pltpu.VMEM((2, tk, tq), F32), pltpu.VMEM((2, 1, tq), F32),
        ],
        compiler_params=_cparams(("parallel", "parallel", "arbitrary")),
        name="attn",
    )(qkv3, qkv3, qkv3, lamp, subw)
    return out.reshape(bsz * seq, d)


def kernel(x, c, ada_w, ada_b, ln1_g, ln1_b, ln2_g, ln2_b, mlp_w1, mlp_w2, ssm_w_in, ssm_conv_w, ssm_conv_b, ssm_dt_bias, ssm_a_log, ssm_d, ssm_norm_w, ssm_w_out, attn_w_qkv, attn_lq1, attn_lk1, attn_lq2, attn_lk2, attn_subln_w, attn_w_out):
    bsz, seq, d = x.shape
    depth = ada_w.shape[0]
    mods = _mods(c, ada_w, ada_b)
    x2 = x.reshape(bsz * seq, d)
    n_main = SSM_GROUPS * GROUP_W * 2 + 2 * SSM_GROUPS * SSM_STATE
    w1_b, w2_b = mlp_w1.astype(BF16), mlp_w2.astype(BF16)
    w_in_b, w_ssm_out_b = ssm_w_in.astype(BF16), ssm_w_out.astype(BF16)
    w_qkv_b, w_attn_out_b = attn_w_qkv.astype(BF16), attn_w_out.astype(BF16)
    for i in range(depth):
        j = i // 2
        if i % 2 == 0:
            heads = ssm_w_in.shape[2] - n_main
            w_dt = jnp.zeros((d, LANE), BF16).at[:, :heads].set(w_in_b[j, :, n_main:])
            zx, dt = _proj(x2, mods[i], 1, 0, w_in_b, j, n_main, BF16, seq, w_dt=w_dt)
            a = _ssd(zx, dt, ssm_conv_w[j], ssm_conv_b[j], ssm_dt_bias[j], ssm_a_log[j], ssm_d[j],
                     ssm_norm_w[j], seq)
            w_out = w_ssm_out_b
        else:
            lambda_init = 0.8 - 0.6 * math.exp(-0.3 * i)
            col_scale = jnp.concatenate([jnp.full((d,), math.log2(math.e) * DIFF_HEAD_DIM ** -0.5, F32),
                                         jnp.ones((2 * d,), F32)])
            qkv = _proj(x2, mods[i], 1, 0, w_qkv_b, j, 3 * d, BF16, seq, col_scale=col_scale)
            a = _attn(qkv, attn_lq1[j], attn_lk1[j], attn_lq2[j], attn_lk2[j], attn_subln_w[j],
                      lambda_init, bsz, seq)
            w_out = w_attn_out_b
        x2 = _outproj(a, w_out, j, x2, mods[i], 2, ln1_g[i], ln1_b[i], seq)
        x2 = _mlp(x2, mods[i], w1_b, w2_b, i, ln2_g[i], ln2_b[i], seq)
    return x2.reshape(bsz, seq, d)
```

```python
import functools
import itertools
import math

import jax
import jax.numpy as jnp
from jax import lax
from jax.experimental import pallas as pl
from jax.experimental.pallas import tpu as pltpu

F32 = jnp.float32
BF16 = jnp.bfloat16

DEPTH = 2
CHUNK = 64
SSM_HEAD_DIM = 64
SSM_GROUPS = 8
SSM_HPG = 4
SSM_STATE = 128
SSM_CONV = 4
GROUP_W = SSM_HPG * SSM_HEAD_DIM
DIFF_HEAD_DIM = 64
DIFF_V_DIM = 128
ATTN_COL_TILE = 256
ATTN_ONES_ROWS = 16
DEEPNORM_ALPHA = (2 * DEPTH) ** 0.25
EPS = 1e-5

LANE = 128
SUBLANE = 8
VMEM_LIMIT = 56 * 1024 * 1024
MLP_TM, MLP_NSUB, MLP_TF = 1024, 4, 1024
OUTPROJ_TM, OUTPROJ_NSUB = 1024, 4
PROJ_TM, PROJ_NSUB, PROJ_TN = 1024, 2, 1024


def _cparams(sem):
    return pltpu.CompilerParams(dimension_semantics=sem, vmem_limit_bytes=VMEM_LIMIT)


def _sigmoid(x):
    return 1.0 / (1.0 + jnp.exp(-x))


def _layer_norm(v, g, b):
    mu = jnp.mean(v, axis=-1, keepdims=True)
    vc = v - mu
    var = jnp.mean(vc * vc, axis=-1, keepdims=True)
    return vc * lax.rsqrt(var + EPS) * g + b


def _dot(a, b):
    return jnp.dot(a, b, preferred_element_type=F32)


def _mods_kernel(c_ref, w_ref, b_ref, o_ref):
    c = c_ref[...]
    cond = c * _sigmoid(c)
    o_ref[0] = _dot(cond.astype(BF16), w_ref[0].astype(BF16)) + b_ref[0]


def _mods(c, ada_w, ada_b):
    depth, d, n = ada_w.shape
    bsz = c.shape[0]
    cp = jnp.zeros((SUBLANE, d), F32).at[:bsz].set(c)
    tn = 1024
    out = pl.pallas_call(
        _mods_kernel,
        grid=(depth, n // tn),
        in_specs=[
            pl.BlockSpec((SUBLANE, d), lambda l, j: (0, 0)),
            pl.BlockSpec((1, d, tn), lambda l, j: (l, 0, j)),
            pl.BlockSpec((1, 1, tn), lambda l, j: (l, 0, j)),
        ],
        out_specs=pl.BlockSpec((1, SUBLANE, tn), lambda l, j: (l, 0, j)),
        out_shape=jax.ShapeDtypeStruct((depth, SUBLANE, n), F32),
        compiler_params=_cparams(("parallel", "parallel")),
        name="mods",
    )(cp, ada_w, ada_b.reshape(depth, 1, n))
    return out[:, :bsz].reshape(depth, bsz * 6, 1, d)


def _proj_body(x_ref, sc_ref, sh_ref, w_ref, o_ref, n_sub, tn, cs_ref=None, wdt_ref=None, dt_ref=None):
    rows = x_ref.shape[0] // n_sub
    n = w_ref.shape[1]
    for s in range(n_sub):
        r0 = s * rows
        hb = (x_ref[r0:r0 + rows, :] * (1.0 + sc_ref[0]) + sh_ref[0]).astype(BF16)
        for j in range(n // tn):
            y = _dot(hb, w_ref[:, j * tn:(j + 1) * tn])
            if cs_ref is not None:
                y = y * cs_ref[:, j * tn:(j + 1) * tn]
            o_ref[r0:r0 + rows, j * tn:(j + 1) * tn] = y.astype(o_ref.dtype)
        if dt_ref is not None:
            dt_ref[r0:r0 + rows, :] = _dot(hb, wdt_ref[...])


def _proj_scaled_kernel(x_ref, sc_ref, sh_ref, w_ref, cs_ref, o_ref, *, n_sub, tn):
    _proj_body(x_ref, sc_ref, sh_ref, w_ref, o_ref, n_sub, tn, cs_ref=cs_ref)


def _proj_dt_kernel(x_ref, sc_ref, sh_ref, w_ref, wdt_ref, o_ref, dt_ref, *, n_sub, tn):
    _proj_body(x_ref, sc_ref, sh_ref, w_ref, o_ref, n_sub, tn, wdt_ref=wdt_ref, dt_ref=dt_ref)


def _proj(x2, mods, k_scale, k_shift, w, layer, n, out_dtype, seq, w_dt=None, col_scale=None):
    t, d = x2.shape
    tm = min(PROJ_TM, seq)
    per_seq = seq // tm
    resident = dict(pipeline_mode=pl.Buffered(1))
    in_specs = [
        pl.BlockSpec((tm, d), lambda i: (i, 0)),
        pl.BlockSpec((1, 1, d), lambda i: ((i // per_seq) * 6 + k_scale, 0, 0)),
        pl.BlockSpec((1, 1, d), lambda i: ((i // per_seq) * 6 + k_shift, 0, 0)),
        pl.BlockSpec((None, d, n), lambda i: (layer, 0, 0), **resident),
    ]
    out_spec = pl.BlockSpec((tm, n), lambda i: (i, 0))
    out_shape = jax.ShapeDtypeStruct((t, n), out_dtype)
    static = dict(n_sub=PROJ_NSUB, tn=min(PROJ_TN, n))
    cp = _cparams(("parallel",))
    if w_dt is None:
        return pl.pallas_call(
            functools.partial(_proj_scaled_kernel, **static),
            grid=(t // tm,),
            in_specs=in_specs + [pl.BlockSpec((1, n), lambda i: (0, 0))],
            out_specs=out_spec,
            out_shape=out_shape,
            compiler_params=cp,
            name="proj",
        )(x2, mods, mods, w, col_scale.reshape(1, n))
    ndt = w_dt.shape[1]
    return pl.pallas_call(
        functools.partial(_proj_dt_kernel, **static),
        grid=(t // tm,),
        in_specs=in_specs + [pl.BlockSpec((d, ndt), lambda i: (0, 0), **resident)],
        out_specs=[out_spec, pl.BlockSpec((tm, ndt), lambda i: (i, 0))],
        out_shape=[out_shape, jax.ShapeDtypeStruct((t, ndt), F32)],
        compiler_params=cp,
        name="proj_dt",
    )(x2, mods, mods, w, w_dt)


def _outproj_kernel(a_ref, w_ref, x_ref, g_ref, lng_ref, lnb_ref, o_ref, *, n_sub):
    rows = x_ref.shape[0] // n_sub
    for s in range(n_sub):
        r0 = s * rows
        y = _dot(a_ref[r0:r0 + rows, :], w_ref[...])
        v = DEEPNORM_ALPHA * x_ref[r0:r0 + rows, :] + g_ref[0] * y
        o_ref[r0:r0 + rows, :] = _layer_norm(v, lng_ref[...], lnb_ref[...])


def _outproj(a, w, layer, x2, mods, k_gate, ln_g, ln_b, seq):
    t, d = x2.shape
    k = a.shape[1]
    tm = min(OUTPROJ_TM, seq)
    per_seq = seq // tm
    return pl.pallas_call(
        functools.partial(_outproj_kernel, n_sub=OUTPROJ_NSUB),
        grid=(t // tm,),
        in_specs=[
            pl.BlockSpec((tm, k), lambda i: (i, 0)),
            pl.BlockSpec((None, k, d), lambda i: (layer, 0, 0), pipeline_mode=pl.Buffered(1)),
            pl.BlockSpec((tm, d), lambda i: (i, 0)),
            pl.BlockSpec((1, 1, d), lambda i: ((i // per_seq) * 6 + k_gate, 0, 0)),
            pl.BlockSpec((1, d), lambda i: (0, 0)),
            pl.BlockSpec((1, d), lambda i: (0, 0)),
        ],
        out_specs=pl.BlockSpec((tm, d), lambda i: (i, 0)),
        out_shape=jax.ShapeDtypeStruct((t, d), F32),
        compiler_params=_cparams(("parallel",)),
        name="outproj",
    )(a, w, x2, mods, ln_g.reshape(1, d), ln_b.reshape(1, d))


def _mlp_kernel(x_ref, sc_ref, sh_ref, g_ref, w1_ref, w2_ref, lng_ref, lnb_ref, o_ref, *, n_sub, tf):
    tm = x_ref.shape[0]
    ff = w1_ref.shape[1]
    rows = tm // n_sub
    for s in range(n_sub):
        r0 = s * rows
        x = x_ref[r0:r0 + rows, :]
        hb = (x * (1.0 + sc_ref[0]) + sh_ref[0]).astype(BF16)
        acc = None
        for f in range(ff // tf):
            u = jnp.maximum(_dot(hb, w1_ref[:, f * tf:(f + 1) * tf]), 0.0)
            part = _dot((u * u).astype(BF16), w2_ref[f * tf:(f + 1) * tf, :])
            acc = part if acc is None else acc + part
        v = DEEPNORM_ALPHA * x + g_ref[0] * acc
        o_ref[r0:r0 + rows, :] = _layer_norm(v, lng_ref[...], lnb_ref[...])


def _mlp(x2, mods, w1, w2, layer, ln_g, ln_b, seq):
    t, d = x2.shape
    ff = w1.shape[2]
    tm = min(MLP_TM, seq)
    per_seq = seq // tm

    def mod_spec(k):
        return pl.BlockSpec((1, 1, d), lambda i: ((i // per_seq) * 6 + k, 0, 0))

    resident = dict(pipeline_mode=pl.Buffered(1))
    return pl.pallas_call(
        functools.partial(_mlp_kernel, n_sub=MLP_NSUB, tf=min(MLP_TF, ff)),
        grid=(t // tm,),
        in_specs=[
            pl.BlockSpec((tm, d), lambda i: (i, 0)),
            mod_spec(4), mod_spec(3), mod_spec(5),
            pl.BlockSpec((None, d, ff), lambda i: (layer, 0, 0), **resident),
            pl.BlockSpec((None, ff, d), lambda i: (layer, 0, 0), **resident),
            pl.BlockSpec((1, d), lambda i: (0, 0)),
            pl.BlockSpec((1, d), lambda i: (0, 0)),
        ],
        out_specs=pl.BlockSpec((tm, d), lambda i: (i, 0)),
        out_shape=jax.ShapeDtypeStruct((t, d), F32),
        compiler_params=_cparams(("parallel",)),
        name="mlp",
    )(x2, mods, mods, mods, w1, w2, ln_g.reshape(1, d), ln_b.reshape(1, d))


def _split3(x):
    hi = x.astype(BF16)
    r1 = x - hi.astype(F32)
    mid = r1.astype(BF16)
    lo = (r1 - mid.astype(F32)).astype(BF16)
    return hi, mid, lo


def _shift_rows(x, prev_tail, k):
    rolled = pltpu.roll(x, k, axis=0)
    top = jnp.where(lax.broadcasted_iota(jnp.int32, prev_tail.shape, 0) < k,
                    pltpu.roll(prev_tail, k, axis=0), rolled[0:SUBLANE])
    return jnp.concatenate([top, rolled[SUBLANE:]], axis=0)


def _ssd_kernel(z_ref, x_ref, b_ref, c_ref, dt_ref, wx_ref, wb_ref, wc_ref, bx_ref, bb_ref, bc_ref,
                dtb_ref, alog_ref, dsk_ref, e_ref, nw_ref, o_ref, tail, state, dt_sp, acs_sp):
    q = CHUNK
    lb = x_ref.shape[0]
    gw = GROUP_W
    ns = SSM_STATE
    g = pl.program_id(2)
    n_chunks = lb // q

    @pl.when(pl.program_id(1) == 0)
    def _():
        tail[g] = jnp.zeros(tail.shape[1:], F32)
        state[g] = jnp.zeros(state.shape[1:], F32)

    @pl.when(g == 0)
    def _():
        dt_raw = dt_ref[...] + dtb_ref[...]
        dt = jnp.maximum(dt_raw, 0.0) + jnp.log(1.0 + jnp.exp(-jnp.abs(dt_raw)))
        a2 = dt * (-jnp.exp(alog_ref[...]) * math.log2(math.e))
        tri = (lax.broadcasted_iota(jnp.int32, (q, q), 0)
               >= lax.broadcasted_iota(jnp.int32, (q, q), 1)).astype(BF16)
        for i, piece in enumerate(_split3(dt)):
            dt_sp[i] = piece
        for c in range(n_chunks):
            acs = sum(_dot(tri, piece) for piece in _split3(a2[c * q:(c + 1) * q]))
            for i, piece in enumerate(_split3(acs)):
                acs_sp[i, c * q:(c + 1) * q, :] = piece

    e = e_ref[0]
    dt_e = sum(_dot(dt_sp[i], e) for i in range(3))
    acs_e = sum(_dot(acs_sp[i], e) for i in range(3))
    d_e = sum(_dot(jnp.broadcast_to(piece, (2 * SUBLANE, LANE)), e) for piece in _split3(dsk_ref[...]))[0:1]
    nw = nw_ref[...]

    xin = jnp.concatenate([x_ref[...], b_ref[...], c_ref[...]], axis=1).astype(F32)
    w = jnp.concatenate([wx_ref[...], wb_ref[...], wc_ref[...]], axis=1)
    conv = jnp.concatenate([bx_ref[...], bb_ref[...], bc_ref[...]], axis=1)
    prev_tail = tail[g]
    conv = conv + w[SSM_CONV - 1:SSM_CONV, :] * xin
    for k in range(1, SSM_CONV):
        conv = conv + w[SSM_CONV - 1 - k:SSM_CONV - k, :] * _shift_rows(xin, prev_tail, k)
    tail[g] = xin[lb - SUBLANE:lb, :]
    xc = conv * _sigmoid(conv)
    xs = xc[:, 0:gw]
    bm = xc[:, gw:gw + ns]
    cm = xc[:, gw + ns:gw + 2 * ns]

    li = lax.broadcasted_iota(jnp.int32, (q, gw), 0)
    si = lax.broadcasted_iota(jnp.int32, (q, gw), 1) & (q - 1)
    diag = li == si
    causal = li >= si
    br = lax.broadcasted_iota(jnp.int32, (gw, gw), 0) // SSM_HEAD_DIM
    bc = lax.broadcasted_iota(jnp.int32, (gw, gw), 1) // SSM_HEAD_DIM
    blockdiag = br == bc

    st = state[g]
    for c in range(n_chunks):
        r0 = c * q
        xs_c = xs[r0:r0 + q]
        bm_c = bm[r0:r0 + q].astype(BF16)
        cm_c = cm[r0:r0 + q].astype(BF16)
        acs = acs_e[r0:r0 + q]
        last = acs[q - 1:q, :]
        arow = jnp.sum(jnp.where(diag, acs, 0.0), axis=0, keepdims=True)
        decay = jnp.exp2(jnp.where(causal, acs - arow, -jnp.inf))
        b4 = jnp.concatenate([bm_c] * SSM_HPG, axis=0)
        cb = lax.dot_general(cm_c, b4, (((1,), (1,)), ((), ())), preferred_element_type=F32)
        m = (cb * decay).astype(BF16)
        xd = xs_c * dt_e[r0:r0 + q]
        xd_b = xd.astype(BF16)
        xbd = jnp.where(blockdiag, jnp.concatenate([xd_b] * SSM_HPG, axis=0), jnp.zeros((), BF16))
        y = _dot(m, xbd)
        y = y + _dot(cm_c, st.astype(BF16)) * jnp.exp2(acs)
        xds = (xd * jnp.exp2(last - acs)).astype(BF16)
        local = lax.dot_general(bm_c, xds, (((0,), (0,)), ((), ())), preferred_element_type=F32)
        st = jnp.exp2(last) * st + local
        y = y + d_e * xs_c
        z_c = z_ref[r0:r0 + q, :].astype(F32)
        gz = y * (z_c * _sigmoid(z_c))
        ms = jnp.mean(gz * gz, axis=-1, keepdims=True)
        o_ref[r0:r0 + q, :] = (gz * lax.rsqrt(ms + EPS) * nw).astype(o_ref.dtype)
    state[g] = st


def _ssd(zx, dt, conv_w, conv_b, dt_bias, a_log, d_skip, norm_w, seq):
    t = zx.shape[0]
    gw, ns, g = GROUP_W, SSM_STATE, SSM_GROUPS
    d_inner = g * gw
    lb = min(2048, seq)
    nj = seq // lb
    bsz = t // seq
    heads = dt_bias.shape[0]

    def pad_lanes(v):
        return jnp.zeros((1, LANE), F32).at[0, :heads].set(v)

    wpad = jnp.zeros((SUBLANE, conv_w.shape[1]), F32).at[:SSM_CONV].set(conv_w)
    cb2 = conv_b.reshape(1, -1)
    hh = jnp.arange(LANE)[None, :, None]
    cc = jnp.arange(gw)[None, None, :] // SSM_HEAD_DIM
    gg = jnp.arange(g)[:, None, None]
    expand = (hh == gg * SSM_HPG + cc).astype(BF16)

    row = lambda b, j, gi: b * nj + j
    xoff, boff, coff = d_inner // gw, (2 * d_inner) // ns, (2 * d_inner + g * ns) // ns
    wb_off, wc_off = d_inner // ns, (d_inner + g * ns) // ns
    in_specs = [
        pl.BlockSpec((lb, gw), lambda b, j, gi: (row(b, j, gi), gi)),
        pl.BlockSpec((lb, gw), lambda b, j, gi: (row(b, j, gi), xoff + gi)),
        pl.BlockSpec((lb, ns), lambda b, j, gi: (row(b, j, gi), boff + gi)),
        pl.BlockSpec((lb, ns), lambda b, j, gi: (row(b, j, gi), coff + gi)),
        pl.BlockSpec((lb, LANE), lambda b, j, gi: (row(b, j, gi), 0)),
        pl.BlockSpec((SUBLANE, gw), lambda b, j, gi: (0, gi)),
        pl.BlockSpec((SUBLANE, ns), lambda b, j, gi: (0, wb_off + gi)),
        pl.BlockSpec((SUBLANE, ns), lambda b, j, gi: (0, wc_off + gi)),
        pl.BlockSpec((1, gw), lambda b, j, gi: (0, gi)),
        pl.BlockSpec((1, ns), lambda b, j, gi: (0, wb_off + gi)),
        pl.BlockSpec((1, ns), lambda b, j, gi: (0, wc_off + gi)),
        pl.BlockSpec((1, LANE), lambda b, j, gi: (0, 0)),
        pl.BlockSpec((1, LANE), lambda b, j, gi: (0, 0)),
        pl.BlockSpec((1, LANE), lambda b, j, gi: (0, 0)),
        pl.BlockSpec((1, LANE, gw), lambda b, j, gi: (gi, 0, 0)),
        pl.BlockSpec((1, gw), lambda b, j, gi: (0, gi)),
    ]
    return pl.pallas_call(
        _ssd_kernel,
        grid=(bsz, nj, g),
        in_specs=in_specs,
        out_specs=pl.BlockSpec((lb, gw), lambda b, j, gi: (row(b, j, gi), gi)),
        out_shape=jax.ShapeDtypeStruct((t, d_inner), BF16),
        scratch_shapes=[
            pltpu.VMEM((g, SUBLANE, gw + 2 * ns), F32),
            pltpu.VMEM((g, ns, gw), F32),
            pltpu.VMEM((3, lb, LANE), BF16),
            pltpu.VMEM((3, lb, LANE), BF16),
        ],
        compiler_params=_cparams(("parallel", "arbitrary", "arbitrary")),
        name="ssd",
    )(zx, zx, zx, zx, dt, wpad, wpad, wpad, cb2, cb2, cb2,
      pad_lanes(dt_bias), pad_lanes(a_log), pad_lanes(d_skip), expand, norm_w.reshape(1, -1))


def _attn_kernel(q_ref, k_ref, v_ref, lamp_ref, subw_ref, o_ref, vt_ref, acc_ref, m_ref,
                 sa_ref, bma_ref, sb_ref, bmb_ref,
                 *, lambda_init):
    tq = q_ref.shape[1]
    tk = vt_ref.shape[2]
    seq = k_ref.shape[1]
    dh = DIFF_HEAD_DIM
    dv = DIFF_V_DIM
    qi = pl.program_id(2)

    @pl.when(qi == 0)
    def _():
        for i in range(seq // tk):
            vt_ref[i, 0:dv, :] = v_ref[0, i * tk:(i + 1) * tk, :].astype(F32).T.astype(BF16)
            vt_ref[i, dv:, :] = jnp.ones((vt_ref.shape[1] - dv, tk), BF16)

    qb = q_ref[0]
    lane = lax.broadcasted_iota(jnp.int32, qb.shape, 1)
    zero = jnp.zeros((), BF16)
    qs = (jnp.where(lane < dh, qb, zero), jnp.where(lane >= dh, qb, zero))

    m_ref[...] = jnp.full(m_ref.shape, -jnp.inf, F32)
    acc_ref[...] = jnp.zeros(acc_ref.shape, F32)

    def scores(kj, s_ref, bm_ref, cols, comp, diag_off=None):
        c0, c1 = cols
        r0 = pl.multiple_of(kj * tk, tk)
        kb = k_ref[0, pl.ds(r0, tk), :]
        s = lax.dot_general(kb, qs[comp][c0:c1], (((1,), (1,)), ((), ())), preferred_element_type=F32)
        if diag_off is not None:
            kc = lax.broadcasted_iota(jnp.int32, s.shape, 0) // CHUNK + diag_off
            qc = (lax.broadcasted_iota(jnp.int32, s.shape, 1) + c0) // CHUNK
            s = jnp.where(kc <= qc, s, -jnp.inf)
        s_ref[comp, :, c0:c1] = s
        bm_ref[comp, :, c0:c1] = jnp.max(s, axis=0, keepdims=True)

    def update(kj, s_ref, bm_ref, cols, comp):
        c0, c1 = cols
        vtb = vt_ref[kj]
        m_old = m_ref[comp, :, c0:c1]
        m_new = jnp.maximum(m_old, bm_ref[comp, :, c0:c1])
        alpha = jnp.exp2(m_old - m_new)
        p = jnp.exp2(s_ref[comp, :, c0:c1] - m_new)
        acc_ref[comp, :, c0:c1] = alpha * acc_ref[comp, :, c0:c1] + _dot(vtb, p.astype(BF16))
        m_ref[comp, :, c0:c1] = m_new

    def score_pieces(kj, buf, diag_off=None, lo=0):
        return [functools.partial(scores, kj, *buf, (t, t + ATTN_COL_TILE), comp, diag_off)
                for t in range(lo, tq, ATTN_COL_TILE) for comp in range(2)]

    def update_pieces(kj, buf, lo=0):
        return [functools.partial(update, kj, *buf, (t, t + ATTN_COL_TILE), comp)
                for t in range(lo, tq, ATTN_COL_TILE) for comp in range(2)]

    def finish(cols):
        c0, c1 = cols
        lp = lamp_ref[...]
        lam = (jnp.exp(jnp.sum(lp[0:1] * lp[1:2], axis=-1, keepdims=True))
               - jnp.exp(jnp.sum(lp[2:3] * lp[3:4], axis=-1, keepdims=True)) + lambda_init)
        o = (acc_ref[0, 0:dv, c0:c1] / acc_ref[0, dv:dv + 1, c0:c1]
             - lam * (acc_ref[1, 0:dv, c0:c1] / acc_ref[1, dv:dv + 1, c0:c1]))
        ms = jnp.mean(o * o, axis=0, keepdims=True)
        sw = jnp.concatenate([subw_ref[...]] * ((c1 - c0) // LANE), axis=1)
        o = o * lax.rsqrt(ms + EPS) * sw * (1.0 - lambda_init)
        o_ref[0, c0:c1, :] = o.T.astype(o_ref.dtype)

    def finish_pieces(lo, hi):
        return [functools.partial(finish, (t, t + ATTN_COL_TILE)) for t in range(lo, hi, ATTN_COL_TILE)]

    def stage(*calls):
        for group in itertools.zip_longest(*calls):
            for call in group:
                if call is not None:
                    call()

    a_buf, b_buf = (sa_ref, bma_ref), (sb_ref, bmb_ref)
    d0 = 2 * qi
    off1 = tk // CHUNK

    @pl.when(qi == 0)
    def _():
        stage(score_pieces(0, a_buf, 0))
        stage(score_pieces(1, b_buf, off1, tk), update_pieces(0, a_buf))
        stage(update_pieces(1, b_buf, tk), finish_pieces(0, tk))
        stage(finish_pieces(tk, tq))

    @pl.when(qi > 0)
    def _():
        stage(score_pieces(0, a_buf))

        def pair(u):
            stage(score_pieces(2 * u + 1, b_buf), update_pieces(2 * u, a_buf))
            stage(score_pieces(2 * u + 2, a_buf), update_pieces(2 * u + 1, b_buf))

        def body(v, carry):
            pair(2 * v)
            pair(2 * v + 1)
            return carry

        n_pairs = qi - 1
        lax.fori_loop(0, n_pairs // 2, body, 0)

        @pl.when(n_pairs % 2 == 1)
        def _():
            pair(n_pairs - 1)

        stage(score_pieces(d0 - 1, b_buf), update_pieces(d0 - 2, a_buf))
        stage(score_pieces(d0, a_buf, 0), update_pieces(d0 - 1, b_buf))
        stage(score_pieces(d0 + 1, b_buf, off1, tk), update_pieces(d0, a_buf))
        stage(update_pieces(d0 + 1, b_buf, tk), finish_pieces(0, tk))
        stage(finish_pieces(tk, tq))


def _attn(qkv, lq1, lk1, lq2, lk2, subln_w, lambda_init, bsz, seq):
    d3 = qkv.shape[1]
    d = d3 // 3
    heads = d // (2 * DIFF_HEAD_DIM)
    tk = min(512, seq // 2)
    tq = 2 * tk
    assert tk % CHUNK == 0 and seq % tq == 0
    qkv3 = qkv.reshape(bsz, seq, d3)
    lamp = jnp.zeros((SUBLANE, LANE), F32)
    for r, v in enumerate((lq1, lk1, lq2, lk2)):
        lamp = lamp.at[r, :v.shape[0]].set(v)
    subw = jnp.broadcast_to(subln_w[:, None], (DIFF_V_DIM, LANE)).astype(F32)
    out = pl.pallas_call(
        functools.partial(_attn_kernel, lambda_init=lambda_init),
        grid=(bsz, heads, seq // tq),
        in_specs=[
            pl.BlockSpec((1, tq, LANE), lambda b, h, i: (b, i, h)),
            pl.BlockSpec((1, seq, LANE), lambda b, h, i: (b, 0, heads + h)),
            pl.BlockSpec((1, seq, LANE), lambda b, h, i: (b, 0, 2 * heads + h)),
            pl.BlockSpec((SUBLANE, LANE), lambda b, h, i: (0, 0)),
            pl.BlockSpec((DIFF_V_DIM, LANE), lambda b, h, i: (0, 0)),
        ],
        out_specs=pl.BlockSpec((1, tq, LANE), lambda b, h, i: (b, i, h)),
        out_shape=jax.ShapeDtypeStruct((bsz, seq, d), BF16),
        scratch_shapes=[
            pltpu.VMEM((seq // tk, DIFF_V_DIM + ATTN_ONES_ROWS, tk), BF16),
            pltpu.VMEM((2, DIFF_V_DIM + ATTN_ONES_ROWS, tq), F32),
            pltpu.VMEM((2, 1, tq), F32),
            pltpu.VMEM((2, tk, tq), F32), pltpu.VMEM((2, 1, tq), F32),
            pltpu.VMEM((2, tk, tq), F32), pltpu.VMEM((2, 1, tq), F32),
        ],
        compiler_params=_cparams(("parallel", "parallel", "arbitrary")),
        name="attn",
    )(qkv3, qkv3, qkv3, lamp, subw)
    return out.reshape(bsz * seq, d)


def kernel(x, c, ada_w, ada_b, ln1_g, ln1_b, ln2_g, ln2_b, mlp_w1, mlp_w2, ssm_w_in, ssm_conv_w, ssm_conv_b, ssm_dt_bias, ssm_a_log, ssm_d, ssm_norm_w, ssm_w_out, attn_w_qkv, attn_lq1, attn_lk1, attn_lq2, attn_lk2, attn_subln_w, attn_w_out):
    bsz, seq, d = x.shape
    depth = ada_w.shape[0]
    mods = _mods(c, ada_w, ada_b)
    x2 = x.reshape(bsz * seq, d)
    n_main = SSM_GROUPS * GROUP_W * 2 + 2 * SSM_GROUPS * SSM_STATE
    w1_b, w2_b = mlp_w1.astype(BF16), mlp_w2.astype(BF16)
    w_in_b, w_ssm_out_b = ssm_w_in.astype(BF16), ssm_w_out.astype(BF16)
    w_qkv_b, w_attn_out_b = attn_w_qkv.astype(BF16), attn_w_out.astype(BF16)
    for i in range(depth):
        j = i // 2
        if i % 2 == 0:
            heads = ssm_w_in.shape[2] - n_main
            w_dt = jnp.zeros((d, LANE), BF16).at[:, :heads].set(w_in_b[j, :, n_main:])
            zx, dt = _proj(x2, mods[i], 1, 0, w_in_b, j, n_main, BF16, seq, w_dt=w_dt)
            a = _ssd(zx, dt, ssm_conv_w[j], ssm_conv_b[j], ssm_dt_bias[j], ssm_a_log[j], ssm_d[j],
                     ssm_norm_w[j], seq)
            w_out = w_ssm_out_b
        else:
            lambda_init = 0.8 - 0.6 * math.exp(-0.3 * i)
            col_scale = jnp.concatenate([jnp.full((d,), math.log2(math.e) * DIFF_HEAD_DIM ** -0.5, F32),
                                         jnp.ones((2 * d,), F32)])
            qkv = _proj(x2, mods[i], 1, 0, w_qkv_b, j, 3 * d, BF16, seq, col_scale=col_scale)
            a = _attn(qkv, attn_lq1[j], attn_lk1[j], attn_lq2[j], attn_lk2[j], attn_subln_w[j],
                      lambda_init, bsz, seq)
            w_out = w_attn_out_b
        x2 = _outproj(a, w_out, j, x2, mods[i], 2, ln1_g[i], ln1_b[i], seq)
        x2 = _mlp(x2, mods[i], w1_b, w2_b, i, ln2_g[i], ln2_b[i], seq)
    return x2.reshape(bsz, seq, d)
```
